```python
import math
import jax
import jax.numpy as jnp
from jax import lax
import numpy as np

D_MODEL = 1024
BATCH = 16
SEQ = 4096
DEPTH = 2

HEAD_DIM = 64
SSD_WIDTH = D_MODEL // 2
SSD_HEADS = SSD_WIDTH // HEAD_DIM
SSD_GROUPS = 2
SSD_STATE = 128
SSD_CONV = 4
SSD_CHUNK = 128
SB_WIDTH = D_MODEL // 4
SB_HEADS = SB_WIDTH // HEAD_DIM
SB_BLOCK = 128
MOBA_WIDTH = D_MODEL // 4
MOBA_HEADS = MOBA_WIDTH // HEAD_DIM
MOBA_BLOCK = 256
MOBA_TOPK = 3
MOBA_Q_CHUNK = 32

MIX_WIDTH = SSD_WIDTH + SB_WIDTH + MOBA_WIDTH
CONV_CH = SSD_WIDTH + 2 * SSD_GROUPS * SSD_STATE
IN_COLS = SSD_WIDTH + CONV_CH + SSD_HEADS + 3 * SB_WIDTH + 3 * MOBA_WIDTH
FFN_HIDDEN = ((8 * D_MODEL + 3 * 256 - 1) // (3 * 256)) * 256
EPS = 1e-6

kernel_name = 'hybrid_ssd_stickbreak_moba_block'


def _rms(x):
    xf = x.astype(jnp.float32)
    return xf * lax.rsqrt(jnp.mean(xf * xf, axis=-1, keepdims=True) + EPS)


def rms_norm(x, w):
    return (_rms(x) * w.astype(jnp.float32)).astype(x.dtype)


def alibi_slopes(n):
    return jnp.asarray(2.0 ** (-8.0 * np.arange(1, n + 1) / n), dtype=jnp.float32)


def _split_heads(t, n_heads):
    b, s, _ = t.shape
    return jnp.transpose(t.reshape(b, s, n_heads, HEAD_DIM), (0, 2, 1, 3)).astype(jnp.float32)


def _merge_heads(t):
    b, h, s, d = t.shape
    return jnp.transpose(t, (0, 2, 1, 3)).reshape(b, s, h * d)


def causal_depthwise_conv(u, w, b):
    k = w.shape[0]
    out = lax.conv_general_dilated(u, w[:, None, :].astype(u.dtype), window_strides=(1,),
                                   padding=[(k - 1, 0)], dimension_numbers=('NWC', 'WIO', 'NWC'),
                                   feature_group_count=u.shape[-1])
    return out + b.astype(u.dtype)


def ssd_chunked(xs, dt, a, bm, cm):
    bsz, s, h, p = xs.shape
    g, n = bm.shape[2], bm.shape[3]
    k = h // g
    q = SSD_CHUNK
    c = s // q
    xdt = (xs * dt[..., None]).reshape(bsz, c, q, g, k, p)
    adt = jnp.moveaxis((dt * a).reshape(bsz, c, q, g, k), 2, -1)
    acum = jnp.cumsum(adt, axis=-1)
    causal = jnp.tril(jnp.ones((q, q), dtype=bool))
    seg = acum[..., :, None] - acum[..., None, :]
    lmat = jnp.exp(jnp.where(causal, seg, -jnp.inf))
    bc = bm.reshape(bsz, c, q, g, n)
    cc = cm.reshape(bsz, c, q, g, n)
    cb = jnp.einsum('bclgn,bcsgn->bcgls', cc, bc)
    y_diag = jnp.einsum('bcgls,bcgkls,bcsgkp->bclgkp', cb, lmat, xdt)
    decay_to_end = jnp.exp(acum[..., -1:] - acum)
    chunk_states = jnp.einsum('bclgn,bcgkl,bclgkp->bcgkpn', bc, decay_to_end, xdt)
    chunk_decay = jnp.exp(acum[..., -1])

    def step(state, inp):
        dec, cs = inp
        return state * dec[..., None, None] + cs, state

    init = jnp.zeros((bsz, g, k, p, n), jnp.float32)
    _, states_in = lax.scan(step, init, (jnp.moveaxis(chunk_decay, 1, 0), jnp.moveaxis(chunk_states, 1, 0)))
    states_in = jnp.moveaxis(states_in, 0, 1)
    y_off = jnp.einsum('bclgn,bcgkpn,bcgkl->bclgkp', cc, states_in, jnp.exp(acum))
    return (y_diag + y_off).reshape(bsz, s, h, p)


def ssd_mixer(z, xbc, dt_raw, conv_w, conv_b, dt_bias, a_log, d_skip, norm_w):
    bsz, s, _ = z.shape
    xbc = jax.nn.silu(causal_depthwise_conv(xbc, conv_w, conv_b)).astype(jnp.float32)
    gn = SSD_GROUPS * SSD_STATE
    xs = xbc[..., :SSD_WIDTH].reshape(bsz, s, SSD_HEADS, HEAD_DIM)
    bm = xbc[..., SSD_WIDTH:SSD_WIDTH + gn].reshape(bsz, s, SSD_GROUPS, SSD_STATE)
    cm = xbc[..., SSD_WIDTH + gn:].reshape(bsz, s, SSD_GROUPS, SSD_STATE)
    dt = jax.nn.softplus(dt_raw.astype(jnp.float32) + dt_bias.astype(jnp.float32))
    a = -jnp.exp(a_log.astype(jnp.float32))
    y = ssd_chunked(xs, dt, a, bm, cm)
    y = y + d_skip.astype(jnp.float32)[:, None] * xs
    y = y.reshape(bsz, s, SSD_WIDTH) * jax.nn.silu(z.astype(jnp.float32))
    y = _rms(y.reshape(bsz, s, SSD_GROUPS, SSD_WIDTH // SSD_GROUPS)).reshape(bsz, s, SSD_WIDTH)
    return (y * norm_w.astype(jnp.float32)).astype(z.dtype)


def stick_breaking_attention(q, k, v):
    bsz, h, s, d = q.shape
    scale = 1.0 / math.sqrt(d)
    outs = []
    for i in range(s // SB_BLOCK):
        start = i * SB_BLOCK
        end = start + SB_BLOCK
        z = jnp.einsum('bhqd,bhkd->bhqk', q[:, :, start:end], k[:, :, :end]) * scale
        tpos = start + jnp.arange(SB_BLOCK)
        spos = jnp.arange(end)
        strict = spos[None, :] < tpos[:, None]
        log_keep = jnp.where(strict, jax.nn.log_sigmoid(-z), 0.0)
        after = lax.cumsum(log_keep, axis=3, reverse=True) - log_keep
        w = jnp.where(strict, jnp.exp(jax.nn.log_sigmoid(z) + after), 0.0)
        outs.append(jnp.einsum('bhqk,bhkd->bhqd', w, v[:, :, :end]))
    return jnp.concatenate(outs, axis=2)


def moba_attention(q, k, v, slopes):
    bsz, h, s, d = q.shape
    blk = MOBA_BLOCK
    nb = -(-s // blk)
    sp = nb * blk
    pad = ((0, 0), (0, 0), (0, sp - s), (0, 0))
    q = jnp.pad(q, pad)
    k = jnp.pad(k, pad)
    v = jnp.pad(v, pad)
    k_blk = k.reshape(bsz, h, nb, blk, d)
    v_blk = v.reshape(bsz, h, nb, blk, d)
    k_mean = jnp.mean(k_blk, axis=3)
    n_sel = min(MOBA_TOPK, nb - 1)
    scale = 1.0 / math.sqrt(d)
    bi = jnp.arange(bsz)[:, None, None, None]
    hi = jnp.arange(h)[None, :, None, None]

    def chunk(ci):
        start = ci * MOBA_Q_CHUNK
        own = start // blk
        qc = lax.dynamic_slice_in_dim(q, start, MOBA_Q_CHUNK, axis=2)
        tpos = start + jnp.arange(MOBA_Q_CHUNK)
        ko = lax.dynamic_slice_in_dim(k, own * blk, blk, axis=2)
        vo = lax.dynamic_slice_in_dim(v, own * blk, blk, axis=2)
        kpos_o = own * blk + jnp.arange(blk)
        dist_o = (tpos[:, None] - kpos_o[None, :]).astype(jnp.float32)
        logit_o = jnp.einsum('bhqd,bhld->bhql', qc, ko) * scale - slopes[None, :, None, None] * dist_o
        logit_o = jnp.where(dist_o >= 0, logit_o, -jnp.inf)
        if n_sel == 0:
            p = jax.nn.softmax(logit_o, axis=-1)
            return jnp.einsum('bhql,bhld->bhqd', p, vo)
        gate = jnp.einsum('bhqd,bhnd->bhqn', qc, k_mean)
        gate = jnp.where(jnp.arange(nb) < own, gate, -jnp.inf)
        _, idx = lax.top_k(gate, n_sel)
        valid = jnp.arange(n_sel) < own
        kg = k_blk[bi, hi, idx]
        vg = v_blk[bi, hi, idx]
        kpos_g = idx[..., None] * blk + jnp.arange(blk)
        dist_g = (tpos[None, None, :, None, None] - kpos_g).astype(jnp.float32)
        logit_g = jnp.einsum('bhqd,bhqjld->bhqjl', qc, kg) * scale - slopes[None, :, None, None, None] * dist_g
        logit_g = jnp.where(valid[:, None], logit_g, -jnp.inf)
        logits = jnp.concatenate([logit_g.reshape(bsz, h, MOBA_Q_CHUNK, n_sel * blk), logit_o], axis=-1)
        p = jax.nn.softmax(logits, axis=-1)
        pg = p[..., :n_sel * blk].reshape(bsz, h, MOBA_Q_CHUNK, n_sel, blk)
        po = p[..., n_sel * blk:]
        return jnp.einsum('bhqjl,bhqjld->bhqd', pg, vg) + jnp.einsum('bhql,bhld->bhqd', po, vo)

    out = lax.map(chunk, jnp.arange(sp // MOBA_Q_CHUNK))
    out = jnp.transpose(out, (1, 2, 0, 3, 4)).reshape(bsz, h, sp, d)
    return out[:, :, :s]


def hybrid_layer(x, pre_mix_norm, w_in, conv_w, conv_b, dt_bias, a_log, d_skip, ssd_norm, sb_norm,
                 moba_norm, w_out, post_mix_norm, pre_ffn_norm, w_gate, w_up, w_down, post_ffn_norm):
    h = rms_norm(x, pre_mix_norm)
    proj = h @ w_in
    o1 = SSD_WIDTH
    o2 = o1 + CONV_CH
    o3 = o2 + SSD_HEADS
    o4 = o3 + 3 * SB_WIDTH
    z, xbc, dt_raw, sb_qkv, moba_qkv = jnp.split(proj, [o1, o2, o3, o4], axis=-1)
    y_ssd = ssd_mixer(z, xbc, dt_raw, conv_w, conv_b, dt_bias, a_log, d_skip, ssd_norm)
    sq, sk, sv = jnp.split(sb_qkv, 3, axis=-1)
    y_sb = _merge_heads(stick_breaking_attention(_split_heads(sq, SB_HEADS), _split_heads(sk, SB_HEADS),
                                                 _split_heads(sv, SB_HEADS)))
    y_sb = (_rms(y_sb) * sb_norm.astype(jnp.float32)).astype(x.dtype)
    mq, mk, mv = jnp.split(moba_qkv, 3, axis=-1)
    y_moba = _merge_heads(moba_attention(_split_heads(mq, MOBA_HEADS), _split_heads(mk, MOBA_HEADS),
                                         _split_heads(mv, MOBA_HEADS), alibi_slopes(MOBA_HEADS)))
    y_moba = (_rms(y_moba) * moba_norm.astype(jnp.float32)).astype(x.dtype)
    mix = jnp.concatenate([y_ssd, y_sb, y_moba], axis=-1) @ w_out
    x = x + rms_norm(mix, post_mix_norm)
    h2 = rms_norm(x, pre_ffn_norm)
    f = (jax.nn.silu(h2 @ w_gate) * (h2 @ w_up)) @ w_down
    return x + rms_norm(f, post_ffn_norm)


def setup_inputs(seed: int = 0) -> dict:
    key = jax.random.key(seed)
    ks = jax.random.split(key, 20)
    f32 = jnp.float32

    def nrm(k, shape, scale):
        return jax.random.normal(k, shape, f32) * scale

    def gain(k, shape):
        return 1.0 + 0.02 * jax.random.normal(k, shape, f32)

    dt0 = jnp.exp(jax.random.uniform(ks[5], (DEPTH, SSD_HEADS), f32, math.log(1e-3), math.log(1e-1)))
    return {
        'x': jax.random.normal(ks[0], (BATCH, SEQ, D_MODEL), f32),
        'pre_mix_norm': gain(ks[1], (DEPTH, D_MODEL)),
        'w_in': nrm(ks[2], (DEPTH, D_MODEL, IN_COLS), D_MODEL ** -0.5),
        'conv_w': nrm(ks[3], (DEPTH, SSD_CONV, CONV_CH), SSD_CONV ** -0.5),
        'conv_b': nrm(ks[4], (DEPTH, CONV_CH), 0.01),
        'dt_bias': dt0 + jnp.log(-jnp.expm1(-dt0)),
        'a_log': jnp.log(jax.random.uniform(ks[6], (DEPTH, SSD_HEADS), f32, 1.0, 16.0)),
        'd_skip': 1.0 + 0.1 * jax.random.normal(ks[7], (DEPTH, SSD_HEADS), f32),
        'ssd_norm': gain(ks[8], (DEPTH, SSD_WIDTH)),
        'sb_norm': gain(ks[9], (DEPTH, SB_WIDTH)),
        'moba_norm': gain(ks[10], (DEPTH, MOBA_WIDTH)),
        'w_out': nrm(ks[11], (DEPTH, MIX_WIDTH, D_MODEL), MIX_WIDTH ** -0.5),
        'post_mix_norm': gain(ks[12], (DEPTH, D_MODEL)),
        'pre_ffn_norm': gain(ks[13], (DEPTH, D_MODEL)),
        'w_gate': nrm(ks[14], (DEPTH, D_MODEL, FFN_HIDDEN), D_MODEL ** -0.5),
        'w_up': nrm(ks[15], (DEPTH, D_MODEL, FFN_HIDDEN), D_MODEL ** -0.5),
        'w_down': nrm(ks[16], (DEPTH, FFN_HIDDEN, D_MODEL), FFN_HIDDEN ** -0.5),
        'post_ffn_norm': gain(ks[17], (DEPTH, D_MODEL)),
    }


def reference(x, pre_mix_norm, w_in, conv_w, conv_b, dt_bias, a_log, d_skip, ssd_norm, sb_norm, moba_norm,
              w_out, post_mix_norm, pre_ffn_norm, w_gate, w_up, w_down, post_ffn_norm):
    for l in range(DEPTH):
        x = hybrid_layer(x, pre_mix_norm[l], w_in[l], conv_w[l], conv_b[l], dt_bias[l], a_log[l], d_skip[l],
                         ssd_norm[l], sb_norm[l], moba_norm[l], w_out[l], post_mix_norm[l], pre_ffn_norm[l],
                         w_gate[l], w_up[l], w_down[l], post_ffn_norm[l])
    return x
```

```python
import functools
import math

import jax
import jax.numpy as jnp
from jax import lax
from jax.experimental import pallas as pl
from jax.experimental.pallas import tpu as pltpu

F32 = jnp.float32
BF16 = jnp.bfloat16

D_MODEL = 1024
HEAD_DIM = 64
SSD_WIDTH = 512
SSD_HEADS = 8
SSD_GROUPS = 2
SSD_STATE = 128
SSD_CONV = 4
SSD_CHUNK = 128
SB_WIDTH = 256
SB_HEADS = 4
MOBA_WIDTH = 256
MOBA_HEADS = 4
MOBA_BLOCK = 256
MOBA_TOPK = 3
CONV_CH = SSD_WIDTH + 2 * SSD_GROUPS * SSD_STATE
FFN_HIDDEN = 2816
EPS = 1e-6

LANES = 128
NEG_BIG = -1e30

COL_XBC = 0
COL_Z = CONV_CH
COL_SB = COL_Z + SSD_WIDTH
COL_MOBA = COL_SB + 3 * SB_WIDTH
PROJ_COLS = COL_MOBA + 3 * MOBA_WIDTH

VMEM_LIMIT = 56 * 1024 * 1024

_NT = (((1,), (1,)), ((), ()))
_TN = (((0,), (0,)), ((), ()))


def _params(n_axes):
    return pltpu.CompilerParams(dimension_semantics=("arbitrary",) * n_axes,
                                vmem_limit_bytes=VMEM_LIMIT)


def _const_spec(shape):
    zeros = (0,) * len(shape)
    return pl.BlockSpec(shape, lambda *_: zeros)


def _dot(a, b):
    return jnp.dot(a, b, preferred_element_type=F32)


def _split3(x):
    hi = x.astype(BF16)
    r1 = x - hi.astype(F32)
    mid = r1.astype(BF16)
    lo = (r1 - mid.astype(F32)).astype(BF16)
    return hi, mid, lo


def _dot_exact_rhs(x, m):
    hi, mid, lo = _split3(x)
    return _dot(hi, m) + _dot(mid, m) + _dot(lo, m)


def _dot_exact_lhs(m, x):
    hi, mid, lo = _split3(x)
    return _dot(m, hi) + _dot(m, mid) + _dot(m, lo)


def _softplus(x):
    return jnp.maximum(x, 0.0) + jnp.log1p(jnp.exp(-jnp.abs(x)))


def _silu(x):
    return x / (1.0 + jnp.exp(-x))


def _rms(x):
    return x * lax.rsqrt(jnp.mean(x * x, axis=-1, keepdims=True) + EPS)


IN_TM = 512
IN_CW = 512


def _in_proj_kernel(x_ref, nw_ref, w_ref, wdt_ref, wdtT_ref, proj_ref, dt_ref, dtT_ref):
    x = x_ref[0]
    h = (_rms(x) * nw_ref[...]).astype(BF16)
    for j in range(PROJ_COLS // IN_CW):
        sl = slice(j * IN_CW, (j + 1) * IN_CW)
        proj_ref[0, :, sl] = _dot(h, w_ref[:, sl]).astype(BF16)
    dt_ref[0] = _dot(h, wdt_ref[...])
    dtT_ref[0] = lax.dot_general(wdtT_ref[...], h, _NT, preferred_element_type=F32)


def _in_proj(x, norm_w, w_main, w_dt, w_dtT):
    b, s, d = x.shape
    tm = min(IN_TM, s)
    return pl.pallas_call(
        _in_proj_kernel,
        grid=(b, s // tm),
        in_specs=[
            pl.BlockSpec((1, tm, d), lambda i, j: (i, j, 0)),
            _const_spec((1, d)),
            _const_spec((d, PROJ_COLS)),
            _const_spec((d, LANES)),
            _const_spec((SSD_HEADS, d)),
        ],
        out_specs=[
            pl.BlockSpec((1, tm, PROJ_COLS), lambda i, j: (i, j, 0)),
            pl.BlockSpec((1, tm, LANES), lambda i, j: (i, j, 0)),
            pl.BlockSpec((1, SSD_HEADS, tm), lambda i, j: (i, 0, j)),
        ],
        out_shape=[
            jax.ShapeDtypeStruct((b, s, PROJ_COLS), BF16),
            jax.ShapeDtypeStruct((b, s, LANES), F32),
            jax.ShapeDtypeStruct((b, SSD_HEADS, s), F32),
        ],
        compiler_params=_params(2),
        name="in_proj",
    )(x, norm_w, w_main, w_dt, w_dtT)


Q = SSD_CHUNK
TAIL = 8
GW = SSD_WIDTH // SSD_GROUPS
GN = SSD_GROUPS * SSD_STATE


def _ssd_kernel(xbc_ref, z_ref, dt_ref, dtT_ref, cw_ref, cb_ref, dtb_ref, alog_ref, dtbT_ref, alogT_ref,
                dskip_ref, nw_ref, tril_ref, triu_ref, e64_ref, e128_ref,
                y_ref, ext_ref, state_ref):
    c = pl.program_id(1)

    @pl.when(c == 0)
    def _():
        ext_ref[0:TAIL, :] = jnp.zeros((TAIL, CONV_CH), F32)
        state_ref[...] = jnp.zeros_like(state_ref)

    ext_ref[TAIL:TAIL + Q, :] = xbc_ref[0].astype(F32)
    conv = cb_ref[...] + cw_ref[0:1, :] * ext_ref[TAIL - 3:TAIL - 3 + Q, :]
    for k in range(1, SSD_CONV):
        conv = conv + cw_ref[k:k + 1, :] * ext_ref[TAIL - 3 + k:TAIL - 3 + k + Q, :]
    new_tail = ext_ref[Q:Q + TAIL, :]
    ext_ref[0:TAIL, :] = new_tail
    xa = _silu(conv)
    xs = xa[:, :SSD_WIDTH]
    bm = xa[:, SSD_WIDTH:SSD_WIDTH + GN]
    cm = xa[:, SSD_WIDTH + GN:]

    dtp = _softplus(dt_ref[0] + dtb_ref[...])
    adt = dtp * (-jnp.exp(alog_ref[...]))
    acum = _dot_exact_lhs(tril_ref[...], adt)
    dt_exp = _dot_exact_rhs(dtp, e64_ref[...])
    acum_exp = _dot_exact_rhs(acum, e64_ref[...])
    acum_b = _dot_exact_rhs(acum, e128_ref[...])
    dtpT = _softplus(dtT_ref[0] + dtbT_ref[...])
    adtT = dtpT * (-jnp.exp(alogT_ref[...]))
    acumT = _dot_exact_rhs(adtT, triu_ref[...])

    xdt = xs * dt_exp
    acum_last = acum_exp[Q - 1:Q, :]
    xdd = (xdt * jnp.exp(acum_last - acum_exp)).astype(BF16)
    in_decay = jnp.exp(acum_exp)
    chunk_decay = jnp.exp(acum_last)

    row = lax.broadcasted_iota(jnp.int32, (Q, Q), 0)
    col = lax.broadcasted_iota(jnp.int32, (Q, Q), 1)
    causal = row >= col
    lane = lax.broadcasted_iota(jnp.int32, (Q, LANES), 1)
    lo_half = lane < HEAD_DIM

    bm16 = bm.astype(BF16)
    cm16 = cm.astype(BF16)
    y_parts = []
    for g in range(SSD_GROUPS):
        bg = bm16[:, g * SSD_STATE:(g + 1) * SSD_STATE]
        cg = cm16[:, g * SSD_STATE:(g + 1) * SSD_STATE]
        cb = lax.dot_general(cg, bg, _NT, preferred_element_type=F32)
        gsl = slice(g * GW, (g + 1) * GW)
        st = state_ref[g]
        y_off = _dot(cg, st.astype(BF16)) * in_decay[:, gsl]
        s_new = lax.dot_general(bg, xdd[:, gsl], _TN, preferred_element_type=F32)
        state_ref[g] = st * chunk_decay[:, gsl] + s_new
        for pp in range(2):
            p = g * 2 + pp
            ms = []
            for hh in range(2):
                h = p * 2 + hh
                seg = acum_b[:, h * LANES:(h + 1) * LANES] - acumT[h:h + 1, :]
                lmat = jnp.exp(jnp.where(causal, seg, -jnp.inf))
                ms.append((cb * lmat).astype(BF16))
            mcat = jnp.concatenate(ms, axis=1)
            xp = xdt[:, p * LANES:(p + 1) * LANES]
            rhs = jnp.concatenate([jnp.where(lo_half, xp, 0.0), jnp.where(lo_half, 0.0, xp)],
                                  axis=0).astype(BF16)
            y_parts.append(_dot(mcat, rhs) + y_off[:, pp * LANES:(pp + 1) * LANES])
    y = jnp.concatenate(y_parts, axis=1) + dskip_ref[...] * xs
    y = y * _silu(z_ref[0].astype(F32))
    y = jnp.concatenate([_rms(y[:, g * GW:(g + 1) * GW]) for g in range(SSD_GROUPS)], axis=1)
    y_ref[0] = (y * nw_ref[...]).astype(BF16)


def _ssd(proj, dt, dtT, conv_w, conv_b, dt_bias, a_log, d_skip, norm_w):
    b, s, _ = proj.shape
    pad8 = lambda v: jnp.pad(v.astype(F32), (0, LANES - SSD_HEADS)).reshape(1, LANES)
    col8 = lambda v: v.astype(F32).reshape(SSD_HEADS, 1)
    ii = jnp.arange(Q)
    tril = (ii[:, None] >= ii[None, :]).astype(BF16)
    triu = (ii[:, None] <= ii[None, :]).astype(BF16)
    hh = jnp.arange(LANES)
    e64 = (hh[:, None] == (jnp.arange(SSD_WIDTH) // HEAD_DIM)[None, :]).astype(BF16)
    e128 = (hh[:, None] == (jnp.arange(SSD_HEADS * LANES) // LANES)[None, :]).astype(BF16)
    dskip = jnp.repeat(d_skip.astype(F32), HEAD_DIM).reshape(1, SSD_WIDTH)
    return pl.pallas_call(
        _ssd_kernel,
        grid=(b, s // Q),
        in_specs=[
            pl.BlockSpec((1, Q, CONV_CH), lambda i, j: (i, j, COL_XBC // CONV_CH)),
            pl.BlockSpec((1, Q, SSD_WIDTH), lambda i, j: (i, j, COL_Z // SSD_WIDTH)),
            pl.BlockSpec((1, Q, LANES), lambda i, j: (i, j, 0)),
            pl.BlockSpec((1, SSD_HEADS, Q), lambda i, j: (i, 0, j)),
            _const_spec((SSD_CONV, CONV_CH)),
            _const_spec((1, CONV_CH)),
            _const_spec((1, LANES)),
            _const_spec((1, LANES)),
            _const_spec((SSD_HEADS, 1)),
            _const_spec((SSD_HEADS, 1)),
            _const_spec((1, SSD_WIDTH)),
            _const_spec((1, SSD_WIDTH)),
            _const_spec((Q, Q)),
            _const_spec((Q, Q)),
            _const_spec((LANES, SSD_WIDTH)),
            _const_spec((LANES, SSD_HEADS * LANES)),
        ],
        out_specs=pl.BlockSpec((1, Q, SSD_WIDTH), lambda i, j: (i, j, 0)),
        out_shape=jax.ShapeDtypeStruct((b, s, SSD_WIDTH), BF16),
        scratch_shapes=[
            pltpu.VMEM((TAIL + Q, CONV_CH), F32),
            pltpu.VMEM((SSD_GROUPS, SSD_STATE, GW), F32),
        ],
        compiler_params=_params(2),
        name="ssd",
    )(proj, proj, dt, dtT, conv_w.astype(F32), conv_b.astype(F32).reshape(1, CONV_CH),
      pad8(dt_bias), pad8(a_log), col8(dt_bias), col8(a_log), dskip,
      norm_w.astype(F32).reshape(1, SSD_WIDTH), tril, triu, e64, e128)


AT = 256
N_AH = 4
ATT_W = N_AH * HEAD_DIM


def _head_masks(rows):
    lane = lax.broadcasted_iota(jnp.int32, (rows, ATT_W), 1)
    return [(lane >= h * HEAD_DIM) & (lane < (h + 1) * HEAD_DIM) for h in range(N_AH)]


def _merge_head_lanes(acc_list, masks):
    out = jnp.where(masks[0], acc_list[0], 0.0)
    for h in range(1, N_AH):
        out = jnp.where(masks[h], acc_list[h], out)
    return out


def _sb_kernel(q_ref, k_ref, v_ref, nw_ref, u_ref, o_ref, qm_ref, acc_ref):
    qi = pl.program_id(1)
    masks = _head_masks(AT)
    q = q_ref[0].astype(F32) * (1.0 / math.sqrt(HEAD_DIM))
    for h in range(N_AH):
        qm_ref[h] = jnp.where(masks[h], q, 0.0).astype(BF16)
    row = lax.broadcasted_iota(jnp.int32, (AT, AT), 0)
    col = lax.broadcasted_iota(jnp.int32, (AT, AT), 1)
    strict = col < row
    u = u_ref[...]

    def block(kb, carries, diag):
        start = pl.multiple_of(kb * AT, AT)
        kblk = k_ref[0, pl.ds(start, AT), :]
        vblk = v_ref[0, pl.ds(start, AT), :]
        new = []
        for h in range(N_AH):
            z = lax.dot_general(qm_ref[h], kblk, _NT, preferred_element_type=F32)
            lk = -_softplus(z)
            if diag:
                lk = jnp.where(strict, lk, 0.0)
            hi = lk.astype(BF16)
            lo = (lk - hi.astype(F32)).astype(BF16)
            cs = _dot(hi, u) + _dot(lo, u)
            logw = z + cs + carries[h]
            if diag:
                logw = jnp.where(strict, logw, NEG_BIG)
            w = jnp.exp(logw).astype(BF16)
            pv = _dot(w, vblk)
            if diag:
                acc_ref[h] = pv
            else:
                acc_ref[h] += pv
            new.append(carries[h] + cs[:, 0:1])
        return tuple(new)

    zero = jnp.zeros((AT, 1), F32)
    carries = block(qi, (zero,) * N_AH, True)
    lax.fori_loop(0, qi, lambda j, cr: block(qi - 1 - j, cr, False), carries)

    y = _merge_head_lanes([acc_ref[h] for h in range(N_AH)], masks)
    o_ref[0] = (_rms(y) * nw_ref[...]).astype(BF16)


def _sb_attention(proj, norm_w):
    b, s, _ = proj.shape
    ii = jnp.arange(AT)
    u = (ii[:, None] >= ii[None, :]).astype(BF16)
    cb = COL_SB // ATT_W
    return pl.pallas_call(
        _sb_kernel,
        grid=(b, s // AT),
        in_specs=[
            pl.BlockSpec((1, AT, ATT_W), lambda i, j: (i, j, cb)),
            pl.BlockSpec((1, s, ATT_W), lambda i, j: (i, 0, cb + 1)),
            pl.BlockSpec((1, s, ATT_W), lambda i, j: (i, 0, cb + 2)),
            _const_spec((1, ATT_W)),
            _const_spec((AT, AT)),
        ],
        out_specs=pl.BlockSpec((1, AT, ATT_W), lambda i, j: (i, j, 0)),
        out_shape=jax.ShapeDtypeStruct((b, s, ATT_W), BF16),
        scratch_shapes=[
            pltpu.VMEM((N_AH, AT, ATT_W), BF16),
            pltpu.VMEM((N_AH, AT, ATT_W), F32),
        ],
        compiler_params=_params(2),
        name="sb_attn",
    )(proj, proj, proj, norm_w.astype(F32).reshape(1, ATT_W), u)


MOBA_SLOPES = tuple(2.0 ** (-8.0 * i / MOBA_HEADS) for i in range(1, MOBA_HEADS + 1))


def _moba_kernel(q_ref, k_ref, v_ref, nw_ref, o_ref, qm_ref, kmean_ref, sel_ref, acc_ref, *, n_blocks):
    qi = pl.program_id(1)

    @pl.when(qi == 0)
    def _():
        kmean_ref[...] = jnp.zeros_like(kmean_ref)
        for n in range(n_blocks):
            kb = k_ref[0, n * AT:(n + 1) * AT, :].astype(F32)
            kmean_ref[n:n + 1, :] = jnp.mean(kb, axis=0, keepdims=True)

    masks = _head_masks(AT)
    q = q_ref[0].astype(F32)
    km_hi = kmean_ref[...].astype(BF16)
    km_lo = (kmean_ref[...] - km_hi.astype(F32)).astype(BF16)
    lane = lax.broadcasted_iota(jnp.int32, (AT, LANES), 1)
    for h in range(N_AH):
        qh = jnp.where(masks[h], q, 0.0).astype(BF16)
        gate = (lax.dot_general(qh, km_hi, _NT, preferred_element_type=F32)
                + lax.dot_general(qh, km_lo, _NT, preferred_element_type=F32))
        gate = jnp.where(lane < qi, gate, -jnp.inf)
        sel = jnp.zeros((AT, LANES), F32)
        for _ in range(MOBA_TOPK):
            m = jnp.max(gate, axis=-1, keepdims=True)
            hit = (gate == m) & (m > -jnp.inf)
            idx = jnp.min(jnp.where(hit, lane, LANES), axis=-1, keepdims=True)
            pick = lane == idx
            sel = jnp.where(pick, 1.0, sel)
            gate = jnp.where(pick, -jnp.inf, gate)
        sel_ref[h] = sel.astype(BF16)
        qm_ref[h] = jnp.where(masks[h], q * (1.0 / math.sqrt(HEAD_DIM)), 0.0).astype(BF16)

    row = lax.broadcasted_iota(jnp.int32, (AT, AT), 0)
    col = lax.broadcasted_iota(jnp.int32, (AT, AT), 1)
    rel = (col - row).astype(F32)
    erow = lax.broadcasted_iota(jnp.int32, (LANES, AT), 0)

    k_own = k_ref[0, pl.ds(pl.multiple_of(qi * AT, AT), AT), :]
    v_own = v_ref[0, pl.ds(pl.multiple_of(qi * AT, AT), AT), :]
    ms, ls = [], []
    for h in range(N_AH):
        s = lax.dot_general(qm_ref[h], k_own, _NT, preferred_element_type=F32) + MOBA_SLOPES[h] * rel
        s = jnp.where(col <= row, s, NEG_BIG)
        m = jnp.max(s, axis=-1, keepdims=True)
        p = jnp.exp(s - m)
        ls.append(jnp.sum(p, axis=-1, keepdims=True))
        ms.append(m)
        acc_ref[h] = _dot(p.astype(BF16), v_own)

    def past(kb, carry):
        ms, ls = carry
        start = pl.multiple_of(kb * AT, AT)
        kblk = k_ref[0, pl.ds(start, AT), :]
        vblk = v_ref[0, pl.ds(start, AT), :]
        expand = (erow == kb).astype(BF16)
        off = ((kb - qi) * AT).astype(F32)
        nm, nl = [], []
        for h in range(N_AH):
            chosen = _dot(sel_ref[h], expand)
            s = (lax.dot_general(qm_ref[h], kblk, _NT, preferred_element_type=F32)
                 + MOBA_SLOPES[h] * (rel + off))
            s = jnp.where(chosen > 0.5, s, NEG_BIG)
            m_new = jnp.maximum(ms[h], jnp.max(s, axis=-1, keepdims=True))
            alpha = jnp.exp(ms[h] - m_new)
            p = jnp.exp(s - m_new)
            nl.append(alpha * ls[h] + jnp.sum(p, axis=-1, keepdims=True))
            nm.append(m_new)
            acc_ref[h] = alpha * acc_ref[h] + _dot(p.astype(BF16), vblk)
        return tuple(nm), tuple(nl)

    ms, ls = lax.fori_loop(0, qi, past, (tuple(ms), tuple(ls)))
    y = _merge_head_lanes([acc_ref[h] / ls[h] for h in range(N_AH)], masks)
    o_ref[0] = (_rms(y) * nw_ref[...]).astype(BF16)


def _moba_attention(proj, norm_w):
    b, s, _ = proj.shape
    n_blocks = s // AT
    assert s % AT == 0 and n_blocks <= LANES
    cb = COL_MOBA // ATT_W
    return pl.pallas_call(
        functools.partial(_moba_kernel, n_blocks=n_blocks),
        grid=(b, n_blocks),
        in_specs=[
            pl.BlockSpec((1, AT, ATT_W), lambda i, j: (i, j, cb)),
            pl.BlockSpec((1, s, ATT_W), lambda i, j: (i, 0, cb + 1)),
            pl.BlockSpec((1, s, ATT_W), lambda i, j: (i, 0, cb + 2)),
            _const_spec((1, ATT_W)),
        ],
        out_specs=pl.BlockSpec((1, AT, ATT_W), lambda i, j: (i, j, 0)),
        out_shape=jax.ShapeDtypeStruct((b, s, ATT_W), BF16),
        scratch_shapes=[
            pltpu.VMEM((N_AH, AT, ATT_W), BF16),
            pltpu.VMEM((LANES, ATT_W), F32),
            pltpu.VMEM((N_AH, AT, LANES), BF16),
            pltpu.VMEM((N_AH, AT, ATT_W), F32),
        ],
        compiler_params=_params(2),
        name="moba_attn",
    )(proj, proj, proj, norm_w.astype(F32).reshape(1, ATT_W))


OUT_TM = 256
FFN_CH = FFN_HIDDEN // 2


def _out_ffn_kernel(x_ref, ys_ref, yb_ref, ym_ref, wo_ref, n1_ref, n2_ref, wg_ref, wu_ref, wd_ref, n3_ref,
                    o_ref):
    mix = (_dot(ys_ref[0], wo_ref[0:SSD_WIDTH, :])
           + _dot(yb_ref[0], wo_ref[SSD_WIDTH:SSD_WIDTH + SB_WIDTH, :])
           + _dot(ym_ref[0], wo_ref[SSD_WIDTH + SB_WIDTH:, :]))
    x1 = x_ref[0] + _rms(mix) * n1_ref[...]
    h2 = (_rms(x1) * n2_ref[...]).astype(BF16)
    f = None
    for j in range(FFN_HIDDEN // FFN_CH):
        sl = slice(j * FFN_CH, (j + 1) * FFN_CH)
        a = (_silu(_dot(h2, wg_ref[:, sl])) * _dot(h2, wu_ref[:, sl])).astype(BF16)
        part = _dot(a, wd_ref[sl, :])
        f = part if f is None else f + part
    o_ref[0] = x1 + _rms(f) * n3_ref[...]


def _out_ffn(x, y_ssd, y_sb, y_moba, w_out, n1, n2, w_gate, w_up, w_down, n3):
    b, s, d = x.shape
    tm = min(OUT_TM, s)
    tile = lambda w: pl.BlockSpec((1, tm, w), lambda i, j: (i, j, 0))
    resident = lambda shape: pl.BlockSpec(shape, lambda i, j: (0, 0), pipeline_mode=pl.Buffered(1))
    vec = lambda v: v.astype(F32).reshape(1, d)
    return pl.pallas_call(
        _out_ffn_kernel,
        grid=(b, s // tm),
        in_specs=[
            tile(d), tile(SSD_WIDTH), tile(SB_WIDTH), tile(MOBA_WIDTH),
            resident((d, d)), _const_spec((1, d)), _const_spec((1, d)),
            resident((d, FFN_HIDDEN)), resident((d, FFN_HIDDEN)), resident((FFN_HIDDEN, d)),
            _const_spec((1, d)),
        ],
        out_specs=tile(d),
        out_shape=jax.ShapeDtypeStruct((b, s, d), F32),
        compiler_params=_params(2),
        name="out_ffn",
    )(x, y_ssd, y_sb, y_moba, w_out, vec(n1), vec(n2), w_gate, w_up, w_down, vec(n3))


def _layer(x, pre_mix_norm, w_in, conv_w, conv_b, dt_bias, a_log, d_skip, ssd_norm, sb_norm, moba_norm,
           w_out, post_mix_norm, pre_ffn_norm, w_gate, w_up, w_down, post_ffn_norm):
    o1 = SSD_WIDTH
    o2 = o1 + CONV_CH
    o3 = o2 + SSD_HEADS
    w_main = jnp.concatenate([w_in[:, o1:o2], w_in[:, :o1], w_in[:, o3:]], axis=1).astype(BF16)
    w_dt = w_in[:, o2:o3].astype(BF16)
    w_dt_pad = jnp.pad(w_dt, ((0, 0), (0, LANES - SSD_HEADS)))
    proj, dt, dtT = _in_proj(x, pre_mix_norm.astype(F32).reshape(1, D_MODEL), w_main, w_dt_pad, w_dt.T)
    y_ssd = _ssd(proj, dt, dtT, conv_w, conv_b, dt_bias, a_log, d_skip, ssd_norm)
    y_sb = _sb_attention(proj, sb_norm)
    y_moba = _moba_attention(proj, moba_norm)
    return _out_ffn(x, y_ssd, y_sb, y_moba, w_out.astype(BF16), post_mix_norm, pre_ffn_norm,
                    w_gate.astype(BF16), w_up.astype(BF16), w_down.astype(BF16), post_ffn_norm)


def kernel(x, pre_mix_norm, w_in, conv_w, conv_b, dt_bias, a_log, d_skip, ssd_norm, sb_norm, moba_norm, w_out,
           post_mix_norm, pre_ffn_norm, w_gate, w_up, w_down, post_ffn_norm):
    depth = w_in.shape[0]
    for l in range(depth):
        x = _layer(x, pre_mix_norm[l], w_in[l], conv_w[l], conv_b[l], dt_bias[l], a_log[l], d_skip[l],
                   ssd_norm[l], sb_norm[l], moba_norm[l], w_out[l], post_mix_norm[l], pre_ffn_norm[l],
                   w_gate[l], w_up[l], w_down[l], post_ffn_norm[l])
    return x
```

```python
import functools
import math

import jax
import jax.numpy as jnp
from jax import lax
from jax.experimental import pallas as pl
from jax.experimental.pallas import tpu as pltpu

F32 = jnp.float32
BF16 = jnp.bfloat16

D_MODEL = 1024
HEAD_DIM = 64
SSD_WIDTH = 512
SSD_HEADS = 8
SSD_GROUPS = 2
SSD_STATE = 128
SSD_CONV = 4
SSD_CHUNK = 128
SB_WIDTH = 256
SB_HEADS = 4
MOBA_WIDTH = 256
MOBA_HEADS = 4
MOBA_BLOCK = 256
MOBA_TOPK = 3
CONV_CH = SSD_WIDTH + 2 * SSD_GROUPS * SSD_STATE
FFN_HIDDEN = 2816
EPS = 1e-6

LANES = 128
NEG_BIG = -1e30

COL_XBC = 0
COL_Z = CONV_CH
COL_SB = COL_Z + SSD_WIDTH
COL_MOBA = COL_SB + 3 * SB_WIDTH
PROJ_COLS = COL_MOBA + 3 * MOBA_WIDTH

VMEM_LIMIT = 56 * 1024 * 1024

_NT = (((1,), (1,)), ((), ()))
_TN = (((0,), (0,)), ((), ()))


def _params(n_axes):
    return pltpu.CompilerParams(dimension_semantics=("arbitrary",) * n_axes,
                                vmem_limit_bytes=VMEM_LIMIT)


def _const_spec(shape):
    zeros = (0,) * len(shape)
    return pl.BlockSpec(shape, lambda *_: zeros)


def _dot(a, b):
    return jnp.dot(a, b, preferred_element_type=F32)


def _dot_nt(a, b):
    return lax.dot_general(a, b, _NT, preferred_element_type=F32)


def _split3(x):
    hi = x.astype(BF16)
    r1 = x - hi.astype(F32)
    mid = r1.astype(BF16)
    lo = (r1 - mid.astype(F32)).astype(BF16)
    return hi, mid, lo


def _dot_exact_rhs(x, m):
    hi, mid, lo = _split3(x)
    return _dot(hi, m) + _dot(mid, m) + _dot(lo, m)


def _dot_exact_lhs(m, x):
    hi, mid, lo = _split3(x)
    return _dot(m, hi) + _dot(m, mid) + _dot(m, lo)


def _softplus(x):
    return jnp.maximum(x, 0.0) + jnp.log1p(jnp.exp(-jnp.abs(x)))


def _silu(x):
    return x / (1.0 + jnp.exp(-x))


def _rms(x):
    return x * lax.rsqrt(jnp.mean(x * x, axis=-1, keepdims=True) + EPS)


IN_TM = 512
IN_CW = 512


def _in_proj_kernel(x_ref, nw_ref, w_ref, wdt_ref, wdtT_ref, proj_ref, dt_ref, dtT_ref):
    x = x_ref[0]
    h = (_rms(x) * nw_ref[...]).astype(BF16)
    for j in range(PROJ_COLS // IN_CW):
        sl = slice(j * IN_CW, (j + 1) * IN_CW)
        proj_ref[0, :, sl] = _dot(h, w_ref[:, sl]).astype(BF16)
    dt_ref[0] = _dot(h, wdt_ref[...])
    dtT_ref[0] = _dot_nt(wdtT_ref[...], h)


def _in_proj(x, norm_w, w_main, w_dt, w_dtT):
    b, s, d = x.shape
    tm = min(IN_TM, s)
    return pl.pallas_call(
        _in_proj_kernel,
        grid=(b, s // tm),
        in_specs=[
            pl.BlockSpec((1, tm, d), lambda i, j: (i, j, 0)),
            _const_spec((1, d)),
            _const_spec((d, PROJ_COLS)),
            _const_spec((d, LANES)),
            _const_spec((SSD_HEADS, d)),
        ],
        out_specs=[
            pl.BlockSpec((1, tm, PROJ_COLS), lambda i, j: (i, j, 0)),
            pl.BlockSpec((1, tm, LANES), lambda i, j: (i, j, 0)),
            pl.BlockSpec((1, SSD_HEADS, tm), lambda i, j: (i, 0, j)),
        ],
        out_shape=[
            jax.ShapeDtypeStruct((b, s, PROJ_COLS), BF16),
            jax.ShapeDtypeStruct((b, s, LANES), F32),
            jax.ShapeDtypeStruct((b, SSD_HEADS, s), F32),
        ],
        compiler_params=_params(2),
        name="in_proj",
    )(x, norm_w, w_main, w_dt, w_dtT)


Q = SSD_CHUNK
TAIL = 8
GW = SSD_WIDTH // SSD_GROUPS
GN = SSD_GROUPS * SSD_STATE


def _ssd_kernel(xbc_ref, z_ref, dt_ref, dtT_ref, cw_ref, cb_ref, dtb_ref, alog_ref, dtbT_ref, alogT_ref,
                dskip_ref, nw_ref, tril_ref, triu_ref, e64_ref, e128_ref,
                y_ref, ext_ref, state_ref):
    c = pl.program_id(1)

    @pl.when(c == 0)
    def _():
        ext_ref[0:TAIL, :] = jnp.zeros((TAIL, CONV_CH), F32)
        state_ref[...] = jnp.zeros_like(state_ref)

    ext_ref[TAIL:TAIL + Q, :] = xbc_ref[0].astype(F32)
    conv = cb_ref[...] + cw_ref[0:1, :] * ext_ref[TAIL - 3:TAIL - 3 + Q, :]
    for k in range(1, SSD_CONV):
        conv = conv + cw_ref[k:k + 1, :] * ext_ref[TAIL - 3 + k:TAIL - 3 + k + Q, :]
    new_tail = ext_ref[Q:Q + TAIL, :]
    ext_ref[0:TAIL, :] = new_tail
    xa = _silu(conv)
    xs = xa[:, :SSD_WIDTH]
    bm = xa[:, SSD_WIDTH:SSD_WIDTH + GN]
    cm = xa[:, SSD_WIDTH + GN:]

    dtp = _softplus(dt_ref[0] + dtb_ref[...])
    adt = dtp * (-jnp.exp(alog_ref[...]))
    acum = _dot_exact_lhs(tril_ref[...], adt)
    dt_exp = _dot_exact_rhs(dtp, e64_ref[...])
    acum_exp = _dot_exact_rhs(acum, e64_ref[...])
    acum_b = _dot_exact_rhs(acum, e128_ref[...])
    dtpT = _softplus(dtT_ref[0] + dtbT_ref[...])
    adtT = dtpT * (-jnp.exp(alogT_ref[...]))
    acumT = _dot_exact_rhs(adtT, triu_ref[...])

    xdt = xs * dt_exp
    acum_last = acum_exp[Q - 1:Q, :]
    xdd = (xdt * jnp.exp(acum_last - acum_exp)).astype(BF16)
    in_decay = jnp.exp(acum_exp)
    chunk_decay = jnp.exp(acum_last)

    row = lax.broadcasted_iota(jnp.int32, (Q, Q), 0)
    col = lax.broadcasted_iota(jnp.int32, (Q, Q), 1)
    causal = row >= col
    lane = lax.broadcasted_iota(jnp.int32, (Q, LANES), 1)
    lo_half = lane < HEAD_DIM

    bm16 = bm.astype(BF16)
    cm16 = cm.astype(BF16)
    y_parts = []
    for g in range(SSD_GROUPS):
        bg = bm16[:, g * SSD_STATE:(g + 1) * SSD_STATE]
        cg = cm16[:, g * SSD_STATE:(g + 1) * SSD_STATE]
        cb = _dot_nt(cg, bg)
        gsl = slice(g * GW, (g + 1) * GW)
        st = state_ref[g]
        y_off = _dot(cg, st.astype(BF16)) * in_decay[:, gsl]
        s_new = lax.dot_general(bg, xdd[:, gsl], _TN, preferred_element_type=F32)
        state_ref[g] = st * chunk_decay[:, gsl] + s_new
        for pp in range(2):
            p = g * 2 + pp
            ms = []
            for hh in range(2):
                h = p * 2 + hh
                seg = acum_b[:, h * LANES:(h + 1) * LANES] - acumT[h:h + 1, :]
                lmat = jnp.exp(jnp.where(causal, seg, -jnp.inf))
                ms.append((cb * lmat).astype(BF16))
            mcat = jnp.concatenate(ms, axis=1)
            xp = xdt[:, p * LANES:(p + 1) * LANES]
            rhs = jnp.concatenate([jnp.where(lo_half, xp, 0.0), jnp.where(lo_half, 0.0, xp)],
                                  axis=0).astype(BF16)
            y_parts.append(_dot(mcat, rhs) + y_off[:, pp * LANES:(pp + 1) * LANES])
    y = jnp.concatenate(y_parts, axis=1) + dskip_ref[...] * xs
    y = y * _silu(z_ref[0].astype(F32))
    y = jnp.concatenate([_rms(y[:, g * GW:(g + 1) * GW]) for g in range(SSD_GROUPS)], axis=1)
    y_ref[0] = (y * nw_ref[...]).astype(BF16)


def _ssd(proj, dt, dtT, conv_w, conv_b, dt_bias, a_log, d_skip, norm_w):
    b, s, _ = proj.shape
    pad8 = lambda v: jnp.pad(v.astype(F32), (0, LANES - SSD_HEADS)).reshape(1, LANES)
    col8 = lambda v: v.astype(F32).reshape(SSD_HEADS, 1)
    ii = jnp.arange(Q)
    tril = (ii[:, None] >= ii[None, :]).astype(BF16)
    triu = (ii[:, None] <= ii[None, :]).astype(BF16)
    hh = jnp.arange(LANES)
    e64 = (hh[:, None] == (jnp.arange(SSD_WIDTH) // HEAD_DIM)[None, :]).astype(BF16)
    e128 = (hh[:, None] == (jnp.arange(SSD_HEADS * LANES) // LANES)[None, :]).astype(BF16)
    dskip = jnp.repeat(d_skip.astype(F32), HEAD_DIM).reshape(1, SSD_WIDTH)
    return pl.pallas_call(
        _ssd_kernel,
        grid=(b, s // Q),
        in_specs=[
            pl.BlockSpec((1, Q, CONV_CH), lambda i, j: (i, j, COL_XBC // CONV_CH)),
            pl.BlockSpec((1, Q, SSD_WIDTH), lambda i, j: (i, j, COL_Z // SSD_WIDTH)),
            pl.BlockSpec((1, Q, LANES), lambda i, j: (i, j, 0)),
            pl.BlockSpec((1, SSD_HEADS, Q), lambda i, j: (i, 0, j)),
            _const_spec((SSD_CONV, CONV_CH)),
            _const_spec((1, CONV_CH)),
            _const_spec((1, LANES)),
            _const_spec((1, LANES)),
            _const_spec((SSD_HEADS, 1)),
            _const_spec((SSD_HEADS, 1)),
            _const_spec((1, SSD_WIDTH)),
            _const_spec((1, SSD_WIDTH)),
            _const_spec((Q, Q)),
            _const_spec((Q, Q)),
            _const_spec((LANES, SSD_WIDTH)),
            _const_spec((LANES, SSD_HEADS * LANES)),
        ],
        out_specs=pl.BlockSpec((1, Q, SSD_WIDTH), lambda i, j: (i, j, 0)),
        out_shape=jax.ShapeDtypeStruct((b, s, SSD_WIDTH), BF16),
        scratch_shapes=[
            pltpu.VMEM((TAIL + Q, CONV_CH), F32),
            pltpu.VMEM((SSD_GROUPS, SSD_STATE, GW), F32),
        ],
        compiler_params=_params(2),
        name="ssd",
    )(proj, proj, dt, dtT, conv_w.astype(F32), conv_b.astype(F32).reshape(1, CONV_CH),
      pad8(dt_bias), pad8(a_log), col8(dt_bias), col8(a_log), dskip,
      norm_w.astype(F32).reshape(1, SSD_WIDTH), tril, triu, e64, e128)


AT = 256
N_AH = 4
ATT_W = N_AH * HEAD_DIM
QK_SCALE = 1.0 / math.sqrt(HEAD_DIM)


def _head_masks(rows):
    lane = lax.broadcasted_iota(jnp.int32, (rows, ATT_W), 1)
    return [(lane >= h * HEAD_DIM) & (lane < (h + 1) * HEAD_DIM) for h in range(N_AH)]


def _merge_columns(cols, masks):
    out = jnp.broadcast_to(cols[N_AH - 1], masks[0].shape)
    for h in range(N_AH - 2, -1, -1):
        out = jnp.where(masks[h], cols[h], out)
    return out


def _fill_block_diag_v(v_ref, vbd_ref, n_blocks, masks):
    def body(n, _):
        start = pl.multiple_of(n * AT, AT)
        vblk = v_ref[0, pl.ds(start, AT), :]
        for h in range(N_AH):
            vbd_ref[n, h * AT:(h + 1) * AT, :] = jnp.where(masks[h], vblk, jnp.zeros_like(vblk))
        return 0
    lax.fori_loop(0, n_blocks, body, 0)


def _sb_kernel(q_ref, k_ref, v_ref, nw_ref, u_ref, o_ref, qm_ref, vbd_ref, acc_ref, *, n_blocks):
    qi = pl.program_id(1)
    masks = _head_masks(AT)

    @pl.when(qi == 0)
    def _():
        _fill_block_diag_v(v_ref, vbd_ref, n_blocks, masks)

    q = q_ref[0].astype(F32) * QK_SCALE
    for h in range(N_AH):
        qm_ref[h] = jnp.where(masks[h], q, 0.0).astype(BF16)
    row = lax.broadcasted_iota(jnp.int32, (AT, AT), 0)
    col = lax.broadcasted_iota(jnp.int32, (AT, AT), 1)
    strict = col < row

    def block(kb, carries, diag):
        start = pl.multiple_of(kb * AT, AT)
        kblk = k_ref[0, pl.ds(start, AT), :]
        u = u_ref[...]
        ws, new = [], []
        for h in range(N_AH):
            z = _dot_nt(qm_ref[h], kblk)
            sp = jnp.maximum(z, 0.0) + jnp.log(1.0 + jnp.exp(-jnp.abs(z)))
            if diag:
                sp = jnp.where(strict, sp, 0.0)
            cs = _dot(sp.astype(BF16), u)
            logw = z - sp - cs - carries[h]
            if diag:
                logw = jnp.where(strict, logw, NEG_BIG)
            ws.append(jnp.exp(logw).astype(BF16))
            new.append(carries[h] + cs[:, 0:1] + sp[:, 0:1])
        pv = _dot(jnp.concatenate(ws, axis=1), vbd_ref[kb])
        if diag:
            acc_ref[...] = pv
        else:
            acc_ref[...] += pv
        return tuple(new)

    zero = jnp.zeros((AT, 1), F32)
    carries = block(qi, (zero,) * N_AH, True)
    lax.fori_loop(0, qi, lambda j, cr: block(qi - 1 - j, cr, False), carries)

    o_ref[0] = (_rms(acc_ref[...]) * nw_ref[...]).astype(BF16)


def _sb_attention(proj, norm_w):
    b, s, _ = proj.shape
    n_blocks = s // AT
    ii = jnp.arange(AT)
    u = (ii[:, None] > ii[None, :]).astype(BF16)
    cb = COL_SB // ATT_W
    return pl.pallas_call(
        functools.partial(_sb_kernel, n_blocks=n_blocks),
        grid=(b, n_blocks),
        in_specs=[
            pl.BlockSpec((1, AT, ATT_W), lambda i, j: (i, j, cb)),
            pl.BlockSpec((1, s, ATT_W), lambda i, j: (i, 0, cb + 1)),
            pl.BlockSpec((1, s, ATT_W), lambda i, j: (i, 0, cb + 2)),
            _const_spec((1, ATT_W)),
            _const_spec((AT, AT)),
        ],
        out_specs=pl.BlockSpec((1, AT, ATT_W), lambda i, j: (i, j, 0)),
        out_shape=jax.ShapeDtypeStruct((b, s, ATT_W), BF16),
        scratch_shapes=[
            pltpu.VMEM((N_AH, AT, ATT_W), BF16),
            pltpu.VMEM((n_blocks, N_AH * AT, ATT_W), BF16),
            pltpu.VMEM((AT, ATT_W), F32),
        ],
        compiler_params=_params(2),
        name="sb_attn",
    )(proj, proj, proj, norm_w.astype(F32).reshape(1, ATT_W), u)


MOBA_SLOPES = tuple(2.0 ** (-8.0 * i / MOBA_HEADS) for i in range(1, MOBA_HEADS + 1))
POS_LANE = LANES - 1


def _moba_kernel(q_ref, k_ref, v_ref, nw_ref, o_ref, qaug_ref, kaug_ref, vbd_ref, kmean_ref, acc_ref,
                 *, n_blocks):
    qi = pl.program_id(1)
    masks = _head_masks(AT)

    @pl.when(qi == 0)
    def _():
        _fill_block_diag_v(v_ref, vbd_ref, n_blocks, masks)
        kmean_ref[...] = jnp.zeros_like(kmean_ref)
        wl = lax.broadcasted_iota(jnp.int32, (AT, LANES), 1)
        wr = lax.broadcasted_iota(jnp.int32, (AT, LANES), 0)

        def body(n, _):
            start = pl.multiple_of(n * AT, AT)
            kblk = k_ref[0, pl.ds(start, AT), :]
            kmean_ref[pl.ds(n, 1), :] = jnp.mean(kblk.astype(F32), axis=0, keepdims=True)
            window = jnp.where(wl == POS_LANE, wr, (wl == n).astype(jnp.int32)).astype(F32).astype(BF16)
            kaug_ref[0, pl.ds(start, AT), :] = jnp.concatenate([kblk[:, :LANES], window], axis=1)
            kaug_ref[1, pl.ds(start, AT), :] = jnp.concatenate([window, kblk[:, LANES:]], axis=1)
            return 0
        lax.fori_loop(0, n_blocks, body, 0)

    q = q_ref[0].astype(F32) * QK_SCALE
    km = kmean_ref[...]
    km_hi = km.astype(BF16)
    km_lo = (km - km_hi.astype(F32)).astype(BF16)
    blk = lax.broadcasted_iota(jnp.int32, (LANES, AT), 0)
    blk_f = blk.astype(F32)
    for h in range(N_AH):
        qh = jnp.where(masks[h], q, 0.0).astype(BF16)
        gate = _dot_nt(km_hi, qh) + _dot_nt(km_lo, qh)
        gate = jnp.where(blk < qi, gate, -jnp.inf)
        chosen = jnp.zeros((LANES, AT), F32)
        for _ in range(MOBA_TOPK):
            m = jnp.max(gate, axis=0, keepdims=True)
            hit = jnp.where(gate == m, blk_f, float(LANES))
            first = jnp.min(hit, axis=0, keepdims=True)
            pick = (blk_f == first) & (m > -jnp.inf)
            chosen = jnp.where(pick, 1.0, chosen)
            gate = jnp.where(pick, -jnp.inf, gate)
        bias = jnp.where((chosen > 0.5) | (blk == qi), 0.0, NEG_BIG)
        bias = jnp.where(blk == POS_LANE, MOBA_SLOPES[h], bias)
        window = bias.T.astype(BF16)
        half = h // 2
        q_half = qh[:, half * LANES:(half + 1) * LANES]
        qaug_ref[h] = jnp.concatenate([q_half, window] if half == 0 else [window, q_half], axis=1)

    row = lax.broadcasted_iota(jnp.int32, (AT, AT), 0)
    col = lax.broadcasted_iota(jnp.int32, (AT, AT), 1)
    causal = col <= row

    start = pl.multiple_of(qi * AT, AT)
    ms, ls, ps = [], [], []
    for h in range(N_AH):
        s = _dot_nt(qaug_ref[h], kaug_ref[h // 2, pl.ds(start, AT), :])
        s = jnp.where(causal, s, NEG_BIG)
        m = jnp.max(s, axis=-1, keepdims=True)
        p = jnp.exp(s - m)
        ls.append(jnp.sum(p, axis=-1, keepdims=True))
        ms.append(m)
        ps.append(p.astype(BF16))
    acc_ref[...] = _dot(jnp.concatenate(ps, axis=1), vbd_ref[qi])

    def past(kb, carry):
        ms, ls = carry
        start = pl.multiple_of(kb * AT, AT)
        dist = ((kb - qi) * AT).astype(F32)
        nm, nl, ps, alphas = [], [], [], []
        for h in range(N_AH):
            c = MOBA_SLOPES[h] * dist
            s = _dot_nt(qaug_ref[h], kaug_ref[h // 2, pl.ds(start, AT), :])
            m_new = jnp.maximum(ms[h], jnp.max(s, axis=-1, keepdims=True) + c)
            alpha = jnp.exp(ms[h] - m_new)
            p = jnp.exp(s - (m_new - c))
            nl.append(alpha * ls[h] + jnp.sum(p, axis=-1, keepdims=True))
            nm.append(m_new)
            ps.append(p.astype(BF16))
            alphas.append(alpha)
        pv = _dot(jnp.concatenate(ps, axis=1), vbd_ref[kb])
        acc_ref[...] = acc_ref[...] * _merge_columns(alphas, masks) + pv
        return tuple(nm), tuple(nl)

    ms, ls = lax.fori_loop(0, qi, past, (tuple(ms), tuple(ls)))
    y = acc_ref[...] / _merge_columns(ls, masks)
    o_ref[0] = (_rms(y) * nw_ref[...]).astype(BF16)


def _moba_attention(proj, norm_w):
    b, s, _ = proj.shape
    n_blocks = s // AT
    assert s % AT == 0 and n_blocks < POS_LANE
    cb = COL_MOBA // ATT_W
    return pl.pallas_call(
        functools.partial(_moba_kernel, n_blocks=n_blocks),
        grid=(b, n_blocks),
        in_specs=[
            pl.BlockSpec((1, AT, ATT_W), lambda i, j: (i, j, cb)),
            pl.BlockSpec((1, s, ATT_W), lambda i, j: (i, 0, cb + 1)),
            pl.BlockSpec((1, s, ATT_W), lambda i, j: (i, 0, cb + 2)),
            _const_spec((1, ATT_W)),
        ],
        out_specs=pl.BlockSpec((1, AT, ATT_W), lambda i, j: (i, j, 0)),
        out_shape=jax.ShapeDtypeStruct((b, s, ATT_W), BF16),
        scratch_shapes=[
            pltpu.VMEM((N_AH, AT, ATT_W), BF16),
            pltpu.VMEM((2, s, ATT_W), BF16),
            pltpu.VMEM((n_blocks, N_AH * AT, ATT_W), BF16),
            pltpu.VMEM((LANES, ATT_W), F32),
            pltpu.VMEM((AT, ATT_W), F32),
        ],
        compiler_params=_params(2),
        name="moba_attn",
    )(proj, proj, proj, norm_w.astype(F32).reshape(1, ATT_W))


OUT_TM = 256
FFN_CH = FFN_HIDDEN // 2


def _out_ffn_kernel(x_ref, ys_ref, yb_ref, ym_ref, wo_ref, n1_ref, n2_ref, wg_ref, wu_ref, wd_ref, n3_ref,
                    o_ref):
    mix = (_dot(ys_ref[0], wo_ref[0:SSD_WIDTH, :])
           + _dot(yb_ref[0], wo_ref[SSD_WIDTH:SSD_WIDTH + SB_WIDTH, :])
           + _dot(ym_ref[0], wo_ref[SSD_WIDTH + SB_WIDTH:, :]))
    x1 = x_ref[0] + _rms(mix) * n1_ref[...]
    h2 = (_rms(x1) * n2_ref[...]).astype(BF16)
    f = None
    for j in range(FFN_HIDDEN // FFN_CH):
        sl = slice(j * FFN_CH, (j + 1) * FFN_CH)
        a = (_silu(_dot(h2, wg_ref[:, sl])) * _dot(h2, wu_ref[:, sl])).astype(BF16)
        part = _dot(a, wd_ref[sl, :])
        f = part if f is None else f + part
    o_ref[0] = x1 + _rms(f) * n3_ref[...]


def _out_ffn(x, y_ssd, y_sb, y_moba, w_out, n1, n2, w_gate, w_up, w_down, n3):
    b, s, d = x.shape
    tm = min(OUT_TM, s)
    tile = lambda w: pl.BlockSpec((1, tm, w), lambda i, j: (i, j, 0))
    resident = lambda shape: pl.BlockSpec(shape, lambda i, j: (0, 0), pipeline_mode=pl.Buffered(1))
    vec = lambda v: v.astype(F32).reshape(1, d)
    return pl.pallas_call(
        _out_ffn_kernel,
        grid=(b, s // tm),
        in_specs=[
            tile(d), tile(SSD_WIDTH), tile(SB_WIDTH), tile(MOBA_WIDTH),
            resident((d, d)), _const_spec((1, d)), _const_spec((1, d)),
            resident((d, FFN_HIDDEN)), resident((d, FFN_HIDDEN)), resident((FFN_HIDDEN, d)),
            _const_spec((1, d)),
        ],
        out_specs=tile(d),
        out_shape=jax.ShapeDtypeStruct((b, s, d), F32),
        compiler_params=_params(2),
        name="out_ffn",
    )(x, y_ssd, y_sb, y_moba, w_out, vec(n1), vec(n2), w_gate, w_up, w_down, vec(n3))


def _layer(x, pre_mix_norm, w_in, conv_w, conv_b, dt_bias, a_log, d_skip, ssd_norm, sb_norm, moba_norm,
           w_out, post_mix_norm, pre_ffn_norm, w_gate, w_up, w_down, post_ffn_norm):
    o1 = SSD_WIDTH
    o2 = o1 + CONV_CH
    o3 = o2 + SSD_HEADS
    w_main = jnp.concatenate([w_in[:, o1:o2], w_in[:, :o1], w_in[:, o3:]], axis=1).astype(BF16)
    w_dt = w_in[:, o2:o3].astype(BF16)
    w_dt_pad = jnp.pad(w_dt, ((0, 0), (0, LANES - SSD_HEADS)))
    proj, dt, dtT = _in_proj(x, pre_mix_norm.astype(F32).reshape(1, D_MODEL), w_main, w_dt_pad, w_dt.T)
    y_ssd = _ssd(proj, dt, dtT, conv_w, conv_b, dt_bias, a_log, d_skip, ssd_norm)
    y_sb = _sb_attention(proj, sb_norm)
    y_moba = _moba_attention(proj, moba_norm)
    return _out_ffn(x, y_ssd, y_sb, y_moba, w_out.astype(BF16), post_mix_norm, pre_ffn_norm,
                    w_gate.astype(BF16), w_up.astype(BF16), w_down.astype(BF16), post_ffn_norm)


def kernel(x, pre_mix_norm, w_in, conv_w, conv_b, dt_bias, a_log, d_skip, ssd_norm, sb_norm, moba_norm, w_out,
           post_mix_norm, pre_ffn_norm, w_gate, w_up, w_down, post_ffn_norm):
    depth = w_in.shape[0]
    for l in range(depth):
        x = _layer(x, pre_mix_norm[l], w_in[l], conv_w[l], conv_b[l], dt_bias[l], a_log[l], d_skip[l],
                   ssd_norm[l], sb_norm[l], moba_norm[l], w_out[l], post_mix_norm[l], pre_ffn_norm[l],
                   w_gate[l], w_up[l], w_down[l], post_ffn_norm[l])
    return x
```

```python
import functools
import math

import jax
import jax.numpy as jnp
from jax import lax
from jax.experimental import pallas as pl
from jax.experimental.pallas import tpu as pltpu

F32 = jnp.float32
BF16 = jnp.bfloat16

D_MODEL = 1024
HEAD_DIM = 64
SSD_WIDTH = 512
SSD_HEADS = 8
SSD_GROUPS = 2
SSD_STATE = 128
SSD_CONV = 4
SSD_CHUNK = 128
SB_WIDTH = 256
SB_HEADS = 4
MOBA_WIDTH = 256
MOBA_HEADS = 4
MOBA_BLOCK = 256
MOBA_TOPK = 3
CONV_CH = SSD_WIDTH + 2 * SSD_GROUPS * SSD_STATE
FFN_HIDDEN = 2816
EPS = 1e-6

LANES = 128
NEG_BIG = -1e30

COL_XBC = 0
COL_Z = CONV_CH
COL_SB = COL_Z + SSD_WIDTH
COL_MOBA = COL_SB + 3 * SB_WIDTH
PROJ_COLS = COL_MOBA + 3 * MOBA_WIDTH

VMEM_LIMIT = 56 * 1024 * 1024

_NT = (((1,), (1,)), ((), ()))
_TN = (((0,), (0,)), ((), ()))


def _params(n_axes):
    return pltpu.CompilerParams(dimension_semantics=("arbitrary",) * n_axes,
                                vmem_limit_bytes=VMEM_LIMIT)


def _const_spec(shape):
    zeros = (0,) * len(shape)
    return pl.BlockSpec(shape, lambda *_: zeros)


def _dot(a, b):
    return jnp.dot(a, b, preferred_element_type=F32)


def _dot_nt(a, b):
    return lax.dot_general(a, b, _NT, preferred_element_type=F32)


def _split3(x):
    hi = x.astype(BF16)
    r1 = x - hi.astype(F32)
    mid = r1.astype(BF16)
    lo = (r1 - mid.astype(F32)).astype(BF16)
    return hi, mid, lo


def _dot_exact_rhs(x, m):
    hi, mid, lo = _split3(x)
    return _dot(hi, m) + _dot(mid, m) + _dot(lo, m)


def _dot_exact_lhs(m, x):
    hi, mid, lo = _split3(x)
    return _dot(m, hi) + _dot(m, mid) + _dot(m, lo)


def _softplus(x):
    return jnp.maximum(x, 0.0) + jnp.log1p(jnp.exp(-jnp.abs(x)))


def _silu(x):
    return x / (1.0 + jnp.exp(-x))


def _rms(x):
    return x * lax.rsqrt(jnp.mean(x * x, axis=-1, keepdims=True) + EPS)


IN_TM = 512
IN_CW = 512


def _in_proj_kernel(x_ref, nw_ref, w_ref, wdt_ref, wdtT_ref, proj_ref, dt_ref, dtT_ref):
    x = x_ref[0]
    h = (_rms(x) * nw_ref[...]).astype(BF16)
    for j in range(PROJ_COLS // IN_CW):
        sl = slice(j * IN_CW, (j + 1) * IN_CW)
        proj_ref[0, :, sl] = _dot(h, w_ref[:, sl]).astype(BF16)
    dt_ref[0] = _dot(h, wdt_ref[...])
    dtT_ref[0] = _dot_nt(wdtT_ref[...], h)


def _in_proj(x, norm_w, w_main, w_dt, w_dtT):
    b, s, d = x.shape
    tm = min(IN_TM, s)
    return pl.pallas_call(
        _in_proj_kernel,
        grid=(b, s // tm),
        in_specs=[
            pl.BlockSpec((1, tm, d), lambda i, j: (i, j, 0)),
            _const_spec((1, d)),
            _const_spec((d, PROJ_COLS)),
            _const_spec((d, LANES)),
            _const_spec((SSD_HEADS, d)),
        ],
        out_specs=[
            pl.BlockSpec((1, tm, PROJ_COLS), lambda i, j: (i, j, 0)),
            pl.BlockSpec((1, tm, LANES), lambda i, j: (i, j, 0)),
            pl.BlockSpec((1, SSD_HEADS, tm), lambda i, j: (i, 0, j)),
        ],
        out_shape=[
            jax.ShapeDtypeStruct((b, s, PROJ_COLS), BF16),
            jax.ShapeDtypeStruct((b, s, LANES), F32),
            jax.ShapeDtypeStruct((b, SSD_HEADS, s), F32),
        ],
        compiler_params=_params(2),
        name="in_proj",
    )(x, norm_w, w_main, w_dt, w_dtT)


Q = SSD_CHUNK
TAIL = 8
GW = SSD_WIDTH // SSD_GROUPS
GN = SSD_GROUPS * SSD_STATE


def _ssd_kernel(xbc_ref, z_ref, dt_ref, dtT_ref, cw_ref, cb_ref, dtb_ref, alog_ref, dtbT_ref, alogT_ref,
                dskip_ref, nw_ref, tril_ref, triu_ref, e64_ref, e128_ref,
                y_ref, ext_ref, state_ref):
    c = pl.program_id(1)

    @pl.when(c == 0)
    def _():
        ext_ref[0:TAIL, :] = jnp.zeros((TAIL, CONV_CH), F32)
        state_ref[...] = jnp.zeros_like(state_ref)

    ext_ref[TAIL:TAIL + Q, :] = xbc_ref[0].astype(F32)
    conv = cb_ref[...] + cw_ref[0:1, :] * ext_ref[TAIL - 3:TAIL - 3 + Q, :]
    for k in range(1, SSD_CONV):
        conv = conv + cw_ref[k:k + 1, :] * ext_ref[TAIL - 3 + k:TAIL - 3 + k + Q, :]
    new_tail = ext_ref[Q:Q + TAIL, :]
    ext_ref[0:TAIL, :] = new_tail
    xa = _silu(conv)
    xs = xa[:, :SSD_WIDTH]
    bm = xa[:, SSD_WIDTH:SSD_WIDTH + GN]
    cm = xa[:, SSD_WIDTH + GN:]

    dtp = _softplus(dt_ref[0] + dtb_ref[...])
    adt = dtp * (-jnp.exp(alog_ref[...]))
    acum = _dot_exact_lhs(tril_ref[...], adt)
    dt_exp = _dot_exact_rhs(dtp, e64_ref[...])
    acum_exp = _dot_exact_rhs(acum, e64_ref[...])
    acum_b = _dot_exact_rhs(acum, e128_ref[...])
    dtpT = _softplus(dtT_ref[0] + dtbT_ref[...])
    adtT = dtpT * (-jnp.exp(alogT_ref[...]))
    acumT = _dot_exact_rhs(adtT, triu_ref[...])

    xdt = xs * dt_exp
    acum_last = acum_exp[Q - 1:Q, :]
    xdd = (xdt * jnp.exp(acum_last - acum_exp)).astype(BF16)
    in_decay = jnp.exp(acum_exp)
    chunk_decay = jnp.exp(acum_last)

    row = lax.broadcasted_iota(jnp.int32, (Q, Q), 0)
    col = lax.broadcasted_iota(jnp.int32, (Q, Q), 1)
    causal = row >= col
    lane = lax.broadcasted_iota(jnp.int32, (Q, LANES), 1)
    lo_half = lane < HEAD_DIM

    bm16 = bm.astype(BF16)
    cm16 = cm.astype(BF16)
    y_parts = []
    for g in range(SSD_GROUPS):
        bg = bm16[:, g * SSD_STATE:(g + 1) * SSD_STATE]
        cg = cm16[:, g * SSD_STATE:(g + 1) * SSD_STATE]
        cb = _dot_nt(cg, bg)
        gsl = slice(g * GW, (g + 1) * GW)
        st = state_ref[g]
        y_off = _dot(cg, st.astype(BF16)) * in_decay[:, gsl]
        s_new = lax.dot_general(bg, xdd[:, gsl], _TN, preferred_element_type=F32)
        state_ref[g] = st * chunk_decay[:, gsl] + s_new
        for pp in range(2):
            p = g * 2 + pp
            ms = []
            for hh in range(2):
                h = p * 2 + hh
                seg = acum_b[:, h * LANES:(h + 1) * LANES] - acumT[h:h + 1, :]
                lmat = jnp.exp(jnp.where(causal, seg, -jnp.inf))
                ms.append((cb * lmat).astype(BF16))
            mcat = jnp.concatenate(ms, axis=1)
            xp = xdt[:, p * LANES:(p + 1) * LANES]
            rhs = jnp.concatenate([jnp.where(lo_half, xp, 0.0), jnp.where(lo_half, 0.0, xp)],
                                  axis=0).astype(BF16)
            y_parts.append(_dot(mcat, rhs) + y_off[:, pp * LANES:(pp + 1) * LANES])
    y = jnp.concatenate(y_parts, axis=1) + dskip_ref[...] * xs
    y = y * _silu(z_ref[0].astype(F32))
    y = jnp.concatenate([_rms(y[:, g * GW:(g + 1) * GW]) for g in range(SSD_GROUPS)], axis=1)
    y_ref[0] = (y * nw_ref[...]).astype(BF16)


def _ssd(proj, dt, dtT, conv_w, conv_b, dt_bias, a_log, d_skip, norm_w):
    b, s, _ = proj.shape
    pad8 = lambda v: jnp.pad(v.astype(F32), (0, LANES - SSD_HEADS)).reshape(1, LANES)
    col8 = lambda v: v.astype(F32).reshape(SSD_HEADS, 1)
    ii = jnp.arange(Q)
    tril = (ii[:, None] >= ii[None, :]).astype(BF16)
    triu = (ii[:, None] <= ii[None, :]).astype(BF16)
    hh = jnp.arange(LANES)
    e64 = (hh[:, None] == (jnp.arange(SSD_WIDTH) // HEAD_DIM)[None, :]).astype(BF16)
    e128 = (hh[:, None] == (jnp.arange(SSD_HEADS * LANES) // LANES)[None, :]).astype(BF16)
    dskip = jnp.repeat(d_skip.astype(F32), HEAD_DIM).reshape(1, SSD_WIDTH)
    return pl.pallas_call(
        _ssd_kernel,
        grid=(b, s // Q),
        in_specs=[
            pl.BlockSpec((1, Q, CONV_CH), lambda i, j: (i, j, COL_XBC // CONV_CH)),
            pl.BlockSpec((1, Q, SSD_WIDTH), lambda i, j: (i, j, COL_Z // SSD_WIDTH)),
            pl.BlockSpec((1, Q, LANES), lambda i, j: (i, j, 0)),
            pl.BlockSpec((1, SSD_HEADS, Q), lambda i, j: (i, 0, j)),
            _const_spec((SSD_CONV, CONV_CH)),
            _const_spec((1, CONV_CH)),
            _const_spec((1, LANES)),
            _const_spec((1, LANES)),
            _const_spec((SSD_HEADS, 1)),
            _const_spec((SSD_HEADS, 1)),
            _const_spec((1, SSD_WIDTH)),
            _const_spec((1, SSD_WIDTH)),
            _const_spec((Q, Q)),
            _const_spec((Q, Q)),
            _const_spec((LANES, SSD_WIDTH)),
            _const_spec((LANES, SSD_HEADS * LANES)),
        ],
        out_specs=pl.BlockSpec((1, Q, SSD_WIDTH), lambda i, j: (i, j, 0)),
        out_shape=jax.ShapeDtypeStruct((b, s, SSD_WIDTH), BF16),
        scratch_shapes=[
            pltpu.VMEM((TAIL + Q, CONV_CH), F32),
            pltpu.VMEM((SSD_GROUPS, SSD_STATE, GW), F32),
        ],
        compiler_params=_params(2),
        name="ssd",
    )(proj, proj, dt, dtT, conv_w.astype(F32), conv_b.astype(F32).reshape(1, CONV_CH),
      pad8(dt_bias), pad8(a_log), col8(dt_bias), col8(a_log), dskip,
      norm_w.astype(F32).reshape(1, SSD_WIDTH), tril, triu, e64, e128)


AT = 256
N_AH = 4
ATT_W = N_AH * HEAD_DIM
QK_SCALE = 1.0 / math.sqrt(HEAD_DIM)
LOG2E = 1.4426950408889634


def _head_masks(rows):
    lane = lax.broadcasted_iota(jnp.int32, (rows, ATT_W), 1)
    return [(lane >= h * HEAD_DIM) & (lane < (h + 1) * HEAD_DIM) for h in range(N_AH)]


def _merge_columns(cols, masks):
    out = jnp.broadcast_to(cols[N_AH - 1], masks[0].shape)
    for h in range(N_AH - 2, -1, -1):
        out = jnp.where(masks[h], cols[h], out)
    return out


def _fill_block_diag(x_ref, bd_ref, n_blocks, masks):
    def body(n, _):
        start = pl.multiple_of(n * AT, AT)
        blk = x_ref[0, pl.ds(start, AT), :]
        for h in range(N_AH):
            bd_ref[n, h * AT:(h + 1) * AT, :] = jnp.where(masks[h], blk, jnp.zeros_like(blk))
        return 0
    lax.fori_loop(0, n_blocks, body, 0)


def _sb_kernel(q_ref, k_ref, v_ref, nw_ref, u_ref, o_ref, kbd_ref, vbd_ref, za_ref, zb_ref, wa_ref, wb_ref,
               acc_ref, *, n_blocks):
    qi = pl.program_id(1)

    @pl.when(qi == 0)
    def _():
        masks = _head_masks(AT)
        _fill_block_diag(k_ref, kbd_ref, n_blocks, masks)
        _fill_block_diag(v_ref, vbd_ref, n_blocks, masks)

    q16 = (q_ref[0].astype(F32) * QK_SCALE).astype(BF16)
    row = lax.broadcasted_iota(jnp.int32, (AT, AT), 0)
    col = lax.broadcasted_iota(jnp.int32, (AT, AT), 1)
    strict = col < row
    heads = [slice(h * AT, (h + 1) * AT) for h in range(N_AH)]

    def scores(kb):
        return _dot_nt(q16, kbd_ref[kb])

    def weights(z_ref, w_ref, carries, diag=False):
        new = []
        for h in range(N_AH):
            z = z_ref[:, heads[h]]
            sp = jnp.maximum(z, 0.0) + jnp.log(1.0 + jnp.exp2(jnp.abs(z) * (-LOG2E)))
            if diag:
                sp = jnp.where(strict, sp, 0.0)
            cs = _dot(sp.astype(BF16), u_ref[...])
            logw = z - sp - cs - carries[h]
            if diag:
                logw = jnp.where(strict, logw, NEG_BIG)
            w_ref[:, heads[h]] = jnp.exp(logw).astype(BF16)
            new.append(carries[h] + cs[:, 0:1] + sp[:, 0:1])
        return tuple(new)

    def add_values(w_ref, kb):
        acc_ref[...] += _dot(w_ref[...], vbd_ref[kb])

    acc_ref[...] = jnp.zeros_like(acc_ref)
    zb_ref[...] = scores(qi)
    za_ref[...] = scores(jnp.maximum(qi - 1, 0))
    zero = jnp.zeros((AT, 1), F32)
    carries = weights(zb_ref, wa_ref, (zero,) * N_AH, diag=True)

    def two_blocks(j, carries):
        kb = qi - 1 - 2 * j
        add_values(wa_ref, kb + 1)
        zb_ref[...] = scores(kb - 1)
        carries = weights(za_ref, wb_ref, carries)
        add_values(wb_ref, kb)
        za_ref[...] = scores(jnp.maximum(kb - 2, 0))
        return weights(zb_ref, wa_ref, carries)

    carries = lax.fori_loop(0, qi // 2, two_blocks, carries)

    @pl.when(qi % 2 == 1)
    def _():
        add_values(wa_ref, 1)
        weights(za_ref, wa_ref, carries)

    y = acc_ref[...] + _dot(wa_ref[...], vbd_ref[0])
    o_ref[0] = (_rms(y) * nw_ref[...]).astype(BF16)


def _sb_attention(proj, norm_w):
    b, s, _ = proj.shape
    n_blocks = s // AT
    ii = jnp.arange(AT)
    u = (ii[:, None] > ii[None, :]).astype(BF16)
    cb = COL_SB // ATT_W
    return pl.pallas_call(
        functools.partial(_sb_kernel, n_blocks=n_blocks),
        grid=(b, n_blocks),
        in_specs=[
            pl.BlockSpec((1, AT, ATT_W), lambda i, j: (i, j, cb)),
            pl.BlockSpec((1, s, ATT_W), lambda i, j: (i, 0, cb + 1)),
            pl.BlockSpec((1, s, ATT_W), lambda i, j: (i, 0, cb + 2)),
            _const_spec((1, ATT_W)),
            _const_spec((AT, AT)),
        ],
        out_specs=pl.BlockSpec((1, AT, ATT_W), lambda i, j: (i, j, 0)),
        out_shape=jax.ShapeDtypeStruct((b, s, ATT_W), BF16),
        scratch_shapes=[
            pltpu.VMEM((n_blocks, N_AH * AT, ATT_W), BF16),
            pltpu.VMEM((n_blocks, N_AH * AT, ATT_W), BF16),
            pltpu.VMEM((AT, N_AH * AT), F32),
            pltpu.VMEM((AT, N_AH * AT), F32),
            pltpu.VMEM((AT, N_AH * AT), BF16),
            pltpu.VMEM((AT, N_AH * AT), BF16),
            pltpu.VMEM((AT, ATT_W), F32),
        ],
        compiler_params=_params(2),
        name="sb_attn",
    )(proj, proj, proj, norm_w.astype(F32).reshape(1, ATT_W), u)


MOBA_SLOPES = tuple(2.0 ** (-8.0 * i / MOBA_HEADS) for i in range(1, MOBA_HEADS + 1))
POS_LANE = LANES - 1
BF16_SUBLANES = 16


def _padded_block_rows(n_blocks):
    return -(-n_blocks // BF16_SUBLANES) * BF16_SUBLANES


def _moba_kernel(q_ref, k_ref, v_ref, nw_ref, o_ref, w_ref, kaug_ref, vt_ref, kmean_ref, kmh_ref, sa_ref,
                 sb_ref, acc_ref, l_ref, *, n_blocks):
    qi = pl.program_id(1)
    nbp = _padded_block_rows(n_blocks)

    @pl.when(qi == 0)
    def _():
        kmean_ref[...] = jnp.zeros_like(kmean_ref)
        wl = lax.broadcasted_iota(jnp.int32, (AT, LANES), 1)
        wr = lax.broadcasted_iota(jnp.int32, (AT, LANES), 0)

        def body(n, _):
            start = pl.multiple_of(n * AT, AT)
            kblk = k_ref[0, pl.ds(start, AT), :]
            kmean_ref[pl.ds(n, 1), :] = jnp.mean(kblk.astype(F32), axis=0, keepdims=True)
            window = jnp.where(wl == POS_LANE, wr, (wl == n).astype(jnp.int32)).astype(F32).astype(BF16)
            kaug_ref[0, pl.ds(start, AT), :] = jnp.concatenate([kblk[:, :LANES], window], axis=1)
            kaug_ref[1, pl.ds(start, AT), :] = jnp.concatenate([window, kblk[:, LANES:]], axis=1)
            vt_ref[n] = v_ref[0, pl.ds(start, AT), :].astype(F32).T.astype(BF16)
            return 0
        lax.fori_loop(0, n_blocks, body, 0)
        km = kmean_ref[0:nbp, :]
        masks = _head_masks(nbp)
        for h in range(N_AH):
            kmh = jnp.where(masks[h], km, 0.0)
            hi = kmh.astype(BF16)
            kmh_ref[h] = hi
            kmh_ref[N_AH + h] = (kmh - hi.astype(F32)).astype(BF16)

    qt = (q_ref[0].astype(F32) * QK_SCALE).T
    qt16 = qt.astype(BF16)
    row = lax.broadcasted_iota(jnp.int32, (LANES, AT), 0)
    blk = lax.broadcasted_iota(jnp.int32, (nbp, AT), 0)
    blk_f = blk.astype(F32)
    tail_row = lax.broadcasted_iota(jnp.int32, (LANES - nbp, AT), 0) + nbp
    for h in range(N_AH):
        gate = _dot(kmh_ref[h], qt16) + _dot(kmh_ref[N_AH + h], qt16)
        gate = jnp.where(blk < qi, gate, -jnp.inf)
        chosen = jnp.zeros((nbp, AT), F32)
        for _ in range(MOBA_TOPK):
            m = jnp.max(gate, axis=0, keepdims=True)
            hit = jnp.where(gate == m, blk_f, float(LANES))
            first = jnp.min(hit, axis=0, keepdims=True)
            pick = (blk_f == first) & (m > -jnp.inf)
            chosen = jnp.where(pick, 1.0, chosen)
            gate = jnp.where(pick, -jnp.inf, gate)
        bias = jnp.where((chosen > 0.5) | (blk == qi), 0.0, NEG_BIG)
        tail = jnp.where(tail_row == POS_LANE, MOBA_SLOPES[h], 0.0)
        half, sub = divmod(h, 2)
        q_half = qt[half * LANES:(half + 1) * LANES, :]
        q_half = jnp.where((row >= sub * HEAD_DIM) & (row < (sub + 1) * HEAD_DIM), q_half, 0.0)
        cols = slice(h * AT, (h + 1) * AT)
        q0, b0 = (0, LANES) if half == 0 else (LANES, 0)
        w_ref[q0:q0 + LANES, cols] = q_half.astype(BF16)
        w_ref[b0:b0 + nbp, cols] = bias.astype(BF16)
        w_ref[b0 + nbp:b0 + LANES, cols] = tail.astype(BF16)

    key = lax.broadcasted_iota(jnp.int32, (AT, AT), 0)
    qry = lax.broadcasted_iota(jnp.int32, (AT, AT), 1)
    causal = key <= qry
    heads = [slice(h * AT, (h + 1) * AT) for h in range(N_AH)]
    head_rows = [slice(h * HEAD_DIM, (h + 1) * HEAD_DIM) for h in range(N_AH)]

    def scores(kb, s_ref):
        start = pl.multiple_of(kb * AT, AT)
        for half in range(2):
            cols = slice(2 * half * AT, 2 * (half + 1) * AT)
            s_ref[:, cols] = _dot(kaug_ref[half, pl.ds(start, AT), :], w_ref[:, cols])

    def update(kb, s_ref, stats, own=False):
        ms, ls = stats
        dist = ((kb - qi) * AT).astype(F32)
        nm, nl, scale, pv = [], [], [], []
        for h in range(N_AH):
            s = s_ref[:, heads[h]]
            if own:
                s = jnp.where(causal, s, NEG_BIG)
                m_new = jnp.max(s, axis=0, keepdims=True)
                p = jnp.exp(s - m_new)
                l_new = jnp.sum(p, axis=0, keepdims=True)
            else:
                c = MOBA_SLOPES[h] * dist
                m_new = jnp.maximum(ms[h], jnp.max(s, axis=0, keepdims=True) + c)
                alpha = jnp.exp(ms[h] - m_new)
                p = jnp.exp(s - (m_new - c))
                l_new = alpha * ls[h] + jnp.sum(p, axis=0, keepdims=True)
                scale.append(jnp.broadcast_to(alpha, (HEAD_DIM, AT)))
            nm.append(m_new)
            nl.append(l_new)
            pv.append(_dot(vt_ref[kb, head_rows[h], :], p.astype(BF16)))
        pv = jnp.concatenate(pv, axis=0)
        if own:
            acc_ref[...] = pv
        else:
            acc_ref[...] = acc_ref[...] * jnp.concatenate(scale, axis=0) + pv
        return tuple(nm), tuple(nl)

    scores(qi, sb_ref)
    scores(0, sa_ref)
    stats = update(qi, sb_ref, (None, None), own=True)

    def two_blocks(j, stats):
        kb = 2 * j
        scores(kb + 1, sb_ref)
        stats = update(kb, sa_ref, stats)
        scores(jnp.minimum(kb + 2, qi), sa_ref)
        return update(kb + 1, sb_ref, stats)

    stats = lax.fori_loop(0, qi // 2, two_blocks, stats)
    for h in range(N_AH):
        l_ref[h:h + 1, :] = stats[1][h]

    @pl.when(qi % 2 == 1)
    def _():
        _, ls_odd = update(qi - 1, sa_ref, stats)
        for h in range(N_AH):
            l_ref[h:h + 1, :] = ls_odd[h]

    yt = jnp.concatenate([acc_ref[head_rows[h], :] / l_ref[h:h + 1, :] for h in range(N_AH)], axis=0)
    o_ref[0] = (_rms(yt.T) * nw_ref[...]).astype(BF16)


def _moba_attention(proj, norm_w):
    b, s, _ = proj.shape
    n_blocks = s // AT
    assert s % AT == 0 and n_blocks < POS_LANE
    cb = COL_MOBA // ATT_W
    return pl.pallas_call(
        functools.partial(_moba_kernel, n_blocks=n_blocks),
        grid=(b, n_blocks),
        in_specs=[
            pl.BlockSpec((1, AT, ATT_W), lambda i, j: (i, j, cb)),
            pl.BlockSpec((1, s, ATT_W), lambda i, j: (i, 0, cb + 1)),
            pl.BlockSpec((1, s, ATT_W), lambda i, j: (i, 0, cb + 2)),
            _const_spec((1, ATT_W)),
        ],
        out_specs=pl.BlockSpec((1, AT, ATT_W), lambda i, j: (i, j, 0)),
        out_shape=jax.ShapeDtypeStruct((b, s, ATT_W), BF16),
        scratch_shapes=[
            pltpu.VMEM((ATT_W, N_AH * AT), BF16),
            pltpu.VMEM((2, s, ATT_W), BF16),
            pltpu.VMEM((n_blocks, ATT_W, AT), BF16),
            pltpu.VMEM((LANES, ATT_W), F32),
            pltpu.VMEM((2 * N_AH, _padded_block_rows(n_blocks), ATT_W), BF16),
            pltpu.VMEM((AT, N_AH * AT), F32),
            pltpu.VMEM((AT, N_AH * AT), F32),
            pltpu.VMEM((ATT_W, AT), F32),
            pltpu.VMEM((2 * N_AH, AT), F32),
        ],
        compiler_params=_params(2),
        name="moba_attn",
    )(proj, proj, proj, norm_w.astype(F32).reshape(1, ATT_W))


OUT_TM = 256
FFN_CH = FFN_HIDDEN // 2


def _out_ffn_kernel(x_ref, ys_ref, yb_ref, ym_ref, wo_ref, n1_ref, n2_ref, wg_ref, wu_ref, wd_ref, n3_ref,
                    o_ref):
    mix = (_dot(ys_ref[0], wo_ref[0:SSD_WIDTH, :])
           + _dot(yb_ref[0], wo_ref[SSD_WIDTH:SSD_WIDTH + SB_WIDTH, :])
           + _dot(ym_ref[0], wo_ref[SSD_WIDTH + SB_WIDTH:, :]))
    x1 = x_ref[0] + _rms(mix) * n1_ref[...]
    h2 = (_rms(x1) * n2_ref[...]).astype(BF16)
    f = None
    for j in range(FFN_HIDDEN // FFN_CH):
        sl = slice(j * FFN_CH, (j + 1) * FFN_CH)
        a = (_silu(_dot(h2, wg_ref[:, sl])) * _dot(h2, wu_ref[:, sl])).astype(BF16)
        part = _dot(a, wd_ref[sl, :])
        f = part if f is None else f + part
    o_ref[0] = x1 + _rms(f) * n3_ref[...]


def _out_ffn(x, y_ssd, y_sb, y_moba, w_out, n1, n2, w_gate, w_up, w_down, n3):
    b, s, d = x.shape
    tm = min(OUT_TM, s)
    tile = lambda w: pl.BlockSpec((1, tm, w), lambda i, j: (i, j, 0))
    resident = lambda shape: pl.BlockSpec(shape, lambda i, j: (0, 0), pipeline_mode=pl.Buffered(1))
    vec = lambda v: v.astype(F32).reshape(1, d)
    return pl.pallas_call(
        _out_ffn_kernel,
        grid=(b, s // tm),
        in_specs=[
            tile(d), tile(SSD_WIDTH), tile(SB_WIDTH), tile(MOBA_WIDTH),
            resident((d, d)), _const_spec((1, d)), _const_spec((1, d)),
            resident((d, FFN_HIDDEN)), resident((d, FFN_HIDDEN)), resident((FFN_HIDDEN, d)),
            _const_spec((1, d)),
        ],
        out_specs=tile(d),
        out_shape=jax.ShapeDtypeStruct((b, s, d), F32),
        compiler_params=_params(2),
        name="out_ffn",
    )(x, y_ssd, y_sb, y_moba, w_out, vec(n1), vec(n2), w_gate, w_up, w_down, vec(n3))


def _layer(x, pre_mix_norm, w_in, conv_w, conv_b, dt_bias, a_log, d_skip, ssd_norm, sb_norm, moba_norm,
           w_out, post_mix_norm, pre_ffn_norm, w_gate, w_up, w_down, post_ffn_norm):
    o1 = SSD_WIDTH
    o2 = o1 + CONV_CH
    o3 = o2 + SSD_HEADS
    w_main = jnp.concatenate([w_in[:, o1:o2], w_in[:, :o1], w_in[:, o3:]], axis=1).astype(BF16)
    w_dt = w_in[:, o2:o3].astype(BF16)
    w_dt_pad = jnp.pad(w_dt, ((0, 0), (0, LANES - SSD_HEADS)))
    proj, dt, dtT = _in_proj(x, pre_mix_norm.astype(F32).reshape(1, D_MODEL), w_main, w_dt_pad, w_dt.T)
    y_ssd = _ssd(proj, dt, dtT, conv_w, conv_b, dt_bias, a_log, d_skip, ssd_norm)
    y_sb = _sb_attention(proj, sb_norm)
    y_moba = _moba_attention(proj, moba_norm)
    return _out_ffn(x, y_ssd, y_sb, y_moba, w_out.astype(BF16), post_mix_norm, pre_ffn_norm,
                    w_gate.astype(BF16), w_up.astype(BF16), w_down.astype(BF16), post_ffn_norm)


def kernel(x, pre_mix_norm, w_in, conv_w, conv_b, dt_bias, a_log, d_skip, ssd_norm, sb_norm, moba_norm, w_out,
           post_mix_norm, pre_ffn_norm, w_gate, w_up, w_down, post_ffn_norm):
    depth = w_in.shape[0]
    for l in range(depth):
        x = _layer(x, pre_mix_norm[l], w_in[l], conv_w[l], conv_b[l], dt_bias[l], a_log[l], d_skip[l],
                   ssd_norm[l], sb_norm[l], moba_norm[l], w_out[l], post_mix_norm[l], pre_ffn_norm[l],
                   w_gate[l], w_up[l], w_down[l], post_ffn_norm[l])
    return x
```

```python
import functools
import math

import jax
import jax.numpy as jnp
from jax import lax
from jax.experimental import pallas as pl
from jax.experimental.pallas import tpu as pltpu

F32 = jnp.float32
BF16 = jnp.bfloat16

D_MODEL = 1024
HEAD_DIM = 64
SSD_WIDTH = 512
SSD_HEADS = 8
SSD_GROUPS = 2
SSD_STATE = 128
SSD_CONV = 4
SSD_CHUNK = 128
SB_WIDTH = 256
SB_HEADS = 4
MOBA_WIDTH = 256
MOBA_HEADS = 4
MOBA_BLOCK = 256
MOBA_TOPK = 3
CONV_CH = SSD_WIDTH + 2 * SSD_GROUPS * SSD_STATE
FFN_HIDDEN = 2816
EPS = 1e-6

LANES = 128
NEG_BIG = -1e30

COL_XBC = 0
COL_Z = CONV_CH
COL_SB = COL_Z + SSD_WIDTH
COL_MOBA = COL_SB + 3 * SB_WIDTH
PROJ_COLS = COL_MOBA + 3 * MOBA_WIDTH

VMEM_LIMIT = 56 * 1024 * 1024

_NT = (((1,), (1,)), ((), ()))
_TN = (((0,), (0,)), ((), ()))


def _params(n_axes):
    return pltpu.CompilerParams(dimension_semantics=("arbitrary",) * n_axes,
                                vmem_limit_bytes=VMEM_LIMIT)


def _const_spec(shape):
    zeros = (0,) * len(shape)
    return pl.BlockSpec(shape, lambda *_: zeros)


def _dot(a, b):
    return jnp.dot(a, b, preferred_element_type=F32)


def _dot_nt(a, b):
    return lax.dot_general(a, b, _NT, preferred_element_type=F32)


def _split3(x):
    hi = x.astype(BF16)
    r1 = x - hi.astype(F32)
    mid = r1.astype(BF16)
    lo = (r1 - mid.astype(F32)).astype(BF16)
    return hi, mid, lo


def _dot_exact_rhs(x, m):
    hi, mid, lo = _split3(x)
    return _dot(hi, m) + _dot(mid, m) + _dot(lo, m)


def _dot_exact_lhs(m, x):
    hi, mid, lo = _split3(x)
    return _dot(m, hi) + _dot(m, mid) + _dot(m, lo)


def _softplus(x):
    return jnp.maximum(x, 0.0) + jnp.log1p(jnp.exp(-jnp.abs(x)))


def _silu(x):
    return x / (1.0 + jnp.exp(-x))


def _rms(x):
    return x * lax.rsqrt(jnp.mean(x * x, axis=-1, keepdims=True) + EPS)


IN_TM = 512
IN_CW = 512


def _in_proj_kernel(x_ref, nw_ref, w_ref, wdt_ref, wdtT_ref, proj_ref, dt_ref, dtT_ref):
    x = x_ref[0]
    h = (_rms(x) * nw_ref[...]).astype(BF16)
    for j in range(PROJ_COLS // IN_CW):
        sl = slice(j * IN_CW, (j + 1) * IN_CW)
        proj_ref[0, :, sl] = _dot(h, w_ref[:, sl]).astype(BF16)
    dt_ref[0] = _dot(h, wdt_ref[...])
    dtT_ref[0] = _dot_nt(wdtT_ref[...], h)


def _in_proj(x, norm_w, w_main, w_dt, w_dtT):
    b, s, d = x.shape
    tm = min(IN_TM, s)
    return pl.pallas_call(
        _in_proj_kernel,
        grid=(b, s // tm),
        in_specs=[
            pl.BlockSpec((1, tm, d), lambda i, j: (i, j, 0)),
            _const_spec((1, d)),
            _const_spec((d, PROJ_COLS)),
            _const_spec((d, LANES)),
            _const_spec((SSD_HEADS, d)),
        ],
        out_specs=[
            pl.BlockSpec((1, tm, PROJ_COLS), lambda i, j: (i, j, 0)),
            pl.BlockSpec((1, tm, LANES), lambda i, j: (i, j, 0)),
            pl.BlockSpec((1, SSD_HEADS, tm), lambda i, j: (i, 0, j)),
        ],
        out_shape=[
            jax.ShapeDtypeStruct((b, s, PROJ_COLS), BF16),
            jax.ShapeDtypeStruct((b, s, LANES), F32),
            jax.ShapeDtypeStruct((b, SSD_HEADS, s), F32),
        ],
        compiler_params=_params(2),
        name="in_proj",
    )(x, norm_w, w_main, w_dt, w_dtT)


Q = SSD_CHUNK
TAIL = 8
GW = SSD_WIDTH // SSD_GROUPS
GN = SSD_GROUPS * SSD_STATE


def _ssd_kernel(xbc_ref, z_ref, dt_ref, dtT_ref, cw_ref, cb_ref, dtb_ref, alog_ref, dtbT_ref, alogT_ref,
                dskip_ref, nw_ref, tril_ref, triu_ref, e64_ref, e128_ref,
                y_ref, ext_ref, state_ref):
    c = pl.program_id(1)

    @pl.when(c == 0)
    def _():
        ext_ref[0:TAIL, :] = jnp.zeros((TAIL, CONV_CH), F32)
        state_ref[...] = jnp.zeros_like(state_ref)

    ext_ref[TAIL:TAIL + Q, :] = xbc_ref[0].astype(F32)
    conv = cb_ref[...] + cw_ref[0:1, :] * ext_ref[TAIL - 3:TAIL - 3 + Q, :]
    for k in range(1, SSD_CONV):
        conv = conv + cw_ref[k:k + 1, :] * ext_ref[TAIL - 3 + k:TAIL - 3 + k + Q, :]
    new_tail = ext_ref[Q:Q + TAIL, :]
    ext_ref[0:TAIL, :] = new_tail
    xa = _silu(conv)
    xs = xa[:, :SSD_WIDTH]
    bm = xa[:, SSD_WIDTH:SSD_WIDTH + GN]
    cm = xa[:, SSD_WIDTH + GN:]

    dtp = _softplus(dt_ref[0] + dtb_ref[...])
    adt = dtp * (-jnp.exp(alog_ref[...]))
    acum = _dot_exact_lhs(tril_ref[...], adt)
    dt_exp = _dot_exact_rhs(dtp, e64_ref[...])
    acum_exp = _dot_exact_rhs(acum, e64_ref[...])
    acum_b = _dot_exact_rhs(acum, e128_ref[...])
    dtpT = _softplus(dtT_ref[0] + dtbT_ref[...])
    adtT = dtpT * (-jnp.exp(alogT_ref[...]))
    acumT = _dot_exact_rhs(adtT, triu_ref[...])

    xdt = xs * dt_exp
    acum_last = acum_exp[Q - 1:Q, :]
    xdd = (xdt * jnp.exp(acum_last - acum_exp)).astype(BF16)
    in_decay = jnp.exp(acum_exp)
    chunk_decay = jnp.exp(acum_last)

    row = lax.broadcasted_iota(jnp.int32, (Q, Q), 0)
    col = lax.broadcasted_iota(jnp.int32, (Q, Q), 1)
    causal = row >= col
    lane = lax.broadcasted_iota(jnp.int32, (Q, LANES), 1)
    lo_half = lane < HEAD_DIM

    bm16 = bm.astype(BF16)
    cm16 = cm.astype(BF16)
    y_parts = []
    for g in range(SSD_GROUPS):
        bg = bm16[:, g * SSD_STATE:(g + 1) * SSD_STATE]
        cg = cm16[:, g * SSD_STATE:(g + 1) * SSD_STATE]
        cb = _dot_nt(cg, bg)
        gsl = slice(g * GW, (g + 1) * GW)
        st = state_ref[g]
        y_off = _dot(cg, st.astype(BF16)) * in_decay[:, gsl]
        s_new = lax.dot_general(bg, xdd[:, gsl], _TN, preferred_element_type=F32)
        state_ref[g] = st * chunk_decay[:, gsl] + s_new
        for pp in range(2):
            p = g * 2 + pp
            ms = []
            for hh in range(2):
                h = p * 2 + hh
                seg = acum_b[:, h * LANES:(h + 1) * LANES] - acumT[h:h + 1, :]
                lmat = jnp.exp(jnp.where(causal, seg, -jnp.inf))
                ms.append((cb * lmat).astype(BF16))
            mcat = jnp.concatenate(ms, axis=1)
            xp = xdt[:, p * LANES:(p + 1) * LANES]
            rhs = jnp.concatenate([jnp.where(lo_half, xp, 0.0), jnp.where(lo_half, 0.0, xp)],
                                  axis=0).astype(BF16)
            y_parts.append(_dot(mcat, rhs) + y_off[:, pp * LANES:(pp + 1) * LANES])
    y = jnp.concatenate(y_parts, axis=1) + dskip_ref[...] * xs
    y = y * _silu(z_ref[0].astype(F32))
    y = jnp.concatenate([_rms(y[:, g * GW:(g + 1) * GW]) for g in range(SSD_GROUPS)], axis=1)
    y_ref[0] = (y * nw_ref[...]).astype(BF16)


def _ssd(proj, dt, dtT, conv_w, conv_b, dt_bias, a_log, d_skip, norm_w):
    b, s, _ = proj.shape
    pad8 = lambda v: jnp.pad(v.astype(F32), (0, LANES - SSD_HEADS)).reshape(1, LANES)
    col8 = lambda v: v.astype(F32).reshape(SSD_HEADS, 1)
    ii = jnp.arange(Q)
    tril = (ii[:, None] >= ii[None, :]).astype(BF16)
    triu = (ii[:, None] <= ii[None, :]).astype(BF16)
    hh = jnp.arange(LANES)
    e64 = (hh[:, None] == (jnp.arange(SSD_WIDTH) // HEAD_DIM)[None, :]).astype(BF16)
    e128 = (hh[:, None] == (jnp.arange(SSD_HEADS * LANES) // LANES)[None, :]).astype(BF16)
    dskip = jnp.repeat(d_skip.astype(F32), HEAD_DIM).reshape(1, SSD_WIDTH)
    return pl.pallas_call(
        _ssd_kernel,
        grid=(b, s // Q),
        in_specs=[
            pl.BlockSpec((1, Q, CONV_CH), lambda i, j: (i, j, COL_XBC // CONV_CH)),
            pl.BlockSpec((1, Q, SSD_WIDTH), lambda i, j: (i, j, COL_Z // SSD_WIDTH)),
            pl.BlockSpec((1, Q, LANES), lambda i, j: (i, j, 0)),
            pl.BlockSpec((1, SSD_HEADS, Q), lambda i, j: (i, 0, j)),
            _const_spec((SSD_CONV, CONV_CH)),
            _const_spec((1, CONV_CH)),
            _const_spec((1, LANES)),
            _const_spec((1, LANES)),
            _const_spec((SSD_HEADS, 1)),
            _const_spec((SSD_HEADS, 1)),
            _const_spec((1, SSD_WIDTH)),
            _const_spec((1, SSD_WIDTH)),
            _const_spec((Q, Q)),
            _const_spec((Q, Q)),
            _const_spec((LANES, SSD_WIDTH)),
            _const_spec((LANES, SSD_HEADS * LANES)),
        ],
        out_specs=pl.BlockSpec((1, Q, SSD_WIDTH), lambda i, j: (i, j, 0)),
        out_shape=jax.ShapeDtypeStruct((b, s, SSD_WIDTH), BF16),
        scratch_shapes=[
            pltpu.VMEM((TAIL + Q, CONV_CH), F32),
            pltpu.VMEM((SSD_GROUPS, SSD_STATE, GW), F32),
        ],
        compiler_params=_params(2),
        name="ssd",
    )(proj, proj, dt, dtT, conv_w.astype(F32), conv_b.astype(F32).reshape(1, CONV_CH),
      pad8(dt_bias), pad8(a_log), col8(dt_bias), col8(a_log), dskip,
      norm_w.astype(F32).reshape(1, SSD_WIDTH), tril, triu, e64, e128)


AT = 256
N_AH = 4
ATT_W = N_AH * HEAD_DIM
QK_SCALE = 1.0 / math.sqrt(HEAD_DIM)
LOG2E = 1.4426950408889634
SIGN_BIT = -2 ** 31


def _head_masks(rows):
    lane = lax.broadcasted_iota(jnp.int32, (rows, ATT_W), 1)
    return [(lane >= h * HEAD_DIM) & (lane < (h + 1) * HEAD_DIM) for h in range(N_AH)]


def _merge_columns(cols, masks):
    out = jnp.broadcast_to(cols[N_AH - 1], masks[0].shape)
    for h in range(N_AH - 2, -1, -1):
        out = jnp.where(masks[h], cols[h], out)
    return out


def _fill_block_diag(x_ref, bd_ref, n_blocks, masks):
    def body(n, _):
        start = pl.multiple_of(n * AT, AT)
        blk = x_ref[0, pl.ds(start, AT), :]
        for h in range(N_AH):
            bd_ref[n, h * AT:(h + 1) * AT, :] = jnp.where(masks[h], blk, jnp.zeros_like(blk))
        return 0
    lax.fori_loop(0, n_blocks, body, 0)


def _sb_kernel(q_ref, k_ref, v_ref, nw_ref, ut_ref, o_ref, wq_ref, vt_ref, za_ref, zb_ref, acc_ref,
               *, n_blocks):
    qi = pl.program_id(1)

    @pl.when(qi == 0)
    def _():
        def body(n, _):
            start = pl.multiple_of(n * AT, AT)
            vt_ref[n] = v_ref[0, pl.ds(start, AT), :].astype(F32).T.astype(BF16)
            return 0
        lax.fori_loop(0, n_blocks, body, 0)

    qt = (q_ref[0].astype(F32) * (QK_SCALE * LOG2E)).T
    lane_row = lax.broadcasted_iota(jnp.int32, (ATT_W, AT), 0)
    heads = [slice(h * AT, (h + 1) * AT) for h in range(N_AH)]
    head_rows = [slice(h * HEAD_DIM, (h + 1) * HEAD_DIM) for h in range(N_AH)]
    for h in range(N_AH):
        own = (lane_row >= h * HEAD_DIM) & (lane_row < (h + 1) * HEAD_DIM)
        wq_ref[:, heads[h]] = jnp.where(own, qt, 0.0).astype(BF16)
    key = lax.broadcasted_iota(jnp.int32, (AT, N_AH * AT), 0)
    qry = lax.broadcasted_iota(jnp.int32, (AT, N_AH * AT), 1) & (AT - 1)
    strict = key < qry

    def scores(kb, z_ref):
        start = pl.multiple_of(kb * AT, AT)
        z_ref[...] = _dot(k_ref[0, pl.ds(start, AT), :], wq_ref[...])

    def step(kb, z_ref, carry, diag=False):
        z = z_ref[...]
        neg_abs = pltpu.bitcast(pltpu.bitcast(z, jnp.int32) | SIGN_BIT, F32)
        sp = jnp.maximum(z, 0.0) + jnp.log(1.0 + jnp.exp2(neg_abs)) * LOG2E
        if diag:
            sp = jnp.where(strict, sp, 0.0)
        cs = _dot(ut_ref[...], sp.astype(BF16))
        logw = z - sp - cs
        if diag:
            logw = jnp.where(strict, logw, NEG_BIG)
        w = jnp.exp2(logw).astype(BF16)
        later = jnp.exp2(-carry)
        pv = jnp.concatenate(
            [_dot(vt_ref[kb, head_rows[h], :], w[:, heads[h]]) * later[:, heads[h]] for h in range(N_AH)],
            axis=0)
        if diag:
            acc_ref[...] = pv
        else:
            acc_ref[...] += pv
        return carry + cs[0:1, :] + sp[0:1, :]

    scores(qi, zb_ref)
    scores(jnp.maximum(qi - 1, 0), za_ref)
    carry = step(qi, zb_ref, jnp.zeros((1, N_AH * AT), F32), diag=True)

    def two_blocks(j, carry):
        kb = qi - 1 - 2 * j
        scores(kb - 1, zb_ref)
        carry = step(kb, za_ref, carry)
        scores(jnp.maximum(kb - 2, 0), za_ref)
        return step(kb - 1, zb_ref, carry)

    carry = lax.fori_loop(0, qi // 2, two_blocks, carry)

    @pl.when(qi % 2 == 1)
    def _():
        step(0, za_ref, carry)

    o_ref[0] = (_rms(acc_ref[...].T) * nw_ref[...]).astype(BF16)


def _sb_attention(proj, norm_w):
    b, s, _ = proj.shape
    n_blocks = s // AT
    ii = jnp.arange(AT)
    ut = (ii[:, None] < ii[None, :]).astype(BF16)
    cb = COL_SB // ATT_W
    return pl.pallas_call(
        functools.partial(_sb_kernel, n_blocks=n_blocks),
        grid=(b, n_blocks),
        in_specs=[
            pl.BlockSpec((1, AT, ATT_W), lambda i, j: (i, j, cb)),
            pl.BlockSpec((1, s, ATT_W), lambda i, j: (i, 0, cb + 1)),
            pl.BlockSpec((1, s, ATT_W), lambda i, j: (i, 0, cb + 2)),
            _const_spec((1, ATT_W)),
            _const_spec((AT, AT)),
        ],
        out_specs=pl.BlockSpec((1, AT, ATT_W), lambda i, j: (i, j, 0)),
        out_shape=jax.ShapeDtypeStruct((b, s, ATT_W), BF16),
        scratch_shapes=[
            pltpu.VMEM((ATT_W, N_AH * AT), BF16),
            pltpu.VMEM((n_blocks, ATT_W, AT), BF16),
            pltpu.VMEM((AT, N_AH * AT), F32),
            pltpu.VMEM((AT, N_AH * AT), F32),
            pltpu.VMEM((ATT_W, AT), F32),
        ],
        compiler_params=_params(2),
        name="sb_attn",
    )(proj, proj, proj, norm_w.astype(F32).reshape(1, ATT_W), ut)


MOBA_SLOPES = tuple(2.0 ** (-8.0 * i / MOBA_HEADS) for i in range(1, MOBA_HEADS + 1))
POS_LANE = LANES - 1
BF16_SUBLANES = 16


def _padded_block_rows(n_blocks):
    return -(-n_blocks // BF16_SUBLANES) * BF16_SUBLANES


def _moba_kernel(q_ref, k_ref, v_ref, nw_ref, o_ref, w_ref, kaug_ref, vt_ref, kmean_ref, kmh_ref, sa_ref,
                 sb_ref, acc_ref, l_ref, *, n_blocks):
    qi = pl.program_id(1)
    nbp = _padded_block_rows(n_blocks)

    @pl.when(qi == 0)
    def _():
        kmean_ref[...] = jnp.zeros_like(kmean_ref)
        wl = lax.broadcasted_iota(jnp.int32, (AT, LANES), 1)
        wr = lax.broadcasted_iota(jnp.int32, (AT, LANES), 0)

        def body(n, _):
            start = pl.multiple_of(n * AT, AT)
            kblk = k_ref[0, pl.ds(start, AT), :]
            kmean_ref[pl.ds(n, 1), :] = jnp.mean(kblk.astype(F32), axis=0, keepdims=True)
            window = jnp.where(wl == POS_LANE, wr, (wl == n).astype(jnp.int32)).astype(F32).astype(BF16)
            kaug_ref[0, pl.ds(start, AT), :] = jnp.concatenate([kblk[:, :LANES], window], axis=1)
            kaug_ref[1, pl.ds(start, AT), :] = jnp.concatenate([window, kblk[:, LANES:]], axis=1)
            vt_ref[n] = v_ref[0, pl.ds(start, AT), :].astype(F32).T.astype(BF16)
            return 0
        lax.fori_loop(0, n_blocks, body, 0)
        km = kmean_ref[0:nbp, :]
        masks = _head_masks(nbp)
        for h in range(N_AH):
            kmh = jnp.where(masks[h], km, 0.0)
            hi = kmh.astype(BF16)
            kmh_ref[h] = hi
            kmh_ref[N_AH + h] = (kmh - hi.astype(F32)).astype(BF16)

    qt = (q_ref[0].astype(F32) * QK_SCALE).T
    qt16 = qt.astype(BF16)
    row = lax.broadcasted_iota(jnp.int32, (LANES, AT), 0)
    blk = lax.broadcasted_iota(jnp.int32, (nbp, AT), 0)
    blk_f = blk.astype(F32)
    tail_row = lax.broadcasted_iota(jnp.int32, (LANES - nbp, AT), 0) + nbp
    for h in range(N_AH):
        gate = _dot(kmh_ref[h], qt16) + _dot(kmh_ref[N_AH + h], qt16)
        gate = jnp.where(blk < qi, gate, -jnp.inf)
        chosen = jnp.zeros((nbp, AT), F32)
        for _ in range(MOBA_TOPK):
            m = jnp.max(gate, axis=0, keepdims=True)
            hit = jnp.where(gate == m, blk_f, float(LANES))
            first = jnp.min(hit, axis=0, keepdims=True)
            pick = (blk_f == first) & (m > -jnp.inf)
            chosen = jnp.where(pick, 1.0, chosen)
            gate = jnp.where(pick, -jnp.inf, gate)
        bias = jnp.where((chosen > 0.5) | (blk == qi), 0.0, NEG_BIG)
        tail = jnp.where(tail_row == POS_LANE, MOBA_SLOPES[h], 0.0)
        half, sub = divmod(h, 2)
        q_half = qt[half * LANES:(half + 1) * LANES, :]
        q_half = jnp.where((row >= sub * HEAD_DIM) & (row < (sub + 1) * HEAD_DIM), q_half, 0.0)
        cols = slice(h * AT, (h + 1) * AT)
        q0, b0 = (0, LANES) if half == 0 else (LANES, 0)
        w_ref[q0:q0 + LANES, cols] = q_half.astype(BF16)
        w_ref[b0:b0 + nbp, cols] = bias.astype(BF16)
        w_ref[b0 + nbp:b0 + LANES, cols] = tail.astype(BF16)

    key = lax.broadcasted_iota(jnp.int32, (AT, AT), 0)
    qry = lax.broadcasted_iota(jnp.int32, (AT, AT), 1)
    causal = key <= qry
    heads = [slice(h * AT, (h + 1) * AT) for h in range(N_AH)]
    head_rows = [slice(h * HEAD_DIM, (h + 1) * HEAD_DIM) for h in range(N_AH)]

    def scores(kb, s_ref):
        start = pl.multiple_of(kb * AT, AT)
        for half in range(2):
            cols = slice(2 * half * AT, 2 * (half + 1) * AT)
            s_ref[:, cols] = _dot(kaug_ref[half, pl.ds(start, AT), :], w_ref[:, cols])

    def update(kb, s_ref, stats, own=False):
        ms, ls = stats
        dist = ((kb - qi) * AT).astype(F32)
        nm, nl, scale, pv = [], [], [], []
        for h in range(N_AH):
            s = s_ref[:, heads[h]]
            if own:
                s = jnp.where(causal, s, NEG_BIG)
                m_new = jnp.max(s, axis=0, keepdims=True)
                p = jnp.exp(s - m_new)
                l_new = jnp.sum(p, axis=0, keepdims=True)
            else:
                c = MOBA_SLOPES[h] * dist
                m_new = jnp.maximum(ms[h], jnp.max(s, axis=0, keepdims=True) + c)
                alpha = jnp.exp(ms[h] - m_new)
                p = jnp.exp(s - (m_new - c))
                l_new = alpha * ls[h] + jnp.sum(p, axis=0, keepdims=True)
                scale.append(jnp.broadcast_to(alpha, (HEAD_DIM, AT)))
            nm.append(m_new)
            nl.append(l_new)
            pv.append(_dot(vt_ref[kb, head_rows[h], :], p.astype(BF16)))
        pv = jnp.concatenate(pv, axis=0)
        if own:
            acc_ref[...] = pv
        else:
            acc_ref[...] = acc_ref[...] * jnp.concatenate(scale, axis=0) + pv
        return tuple(nm), tuple(nl)

    scores(qi, sb_ref)
    scores(0, sa_ref)
    stats = update(qi, sb_ref, (None, None), own=True)

    def two_blocks(j, stats):
        kb = 2 * j
        scores(kb + 1, sb_ref)
        stats = update(kb, sa_ref, stats)
        scores(jnp.minimum(kb + 2, qi), sa_ref)
        return update(kb + 1, sb_ref, stats)

    stats = lax.fori_loop(0, qi // 2, two_blocks, stats)
    for h in range(N_AH):
        l_ref[h:h + 1, :] = stats[1][h]

    @pl.when(qi % 2 == 1)
    def _():
        _, ls_odd = update(qi - 1, sa_ref, stats)
        for h in range(N_AH):
            l_ref[h:h + 1, :] = ls_odd[h]

    yt = jnp.concatenate([acc_ref[head_rows[h], :] / l_ref[h:h + 1, :] for h in range(N_AH)], axis=0)
    o_ref[0] = (_rms(yt.T) * nw_ref[...]).astype(BF16)


def _moba_attention(proj, norm_w):
    b, s, _ = proj.shape
    n_blocks = s // AT
    assert s % AT == 0 and n_blocks < POS_LANE
    cb = COL_MOBA // ATT_W
    return pl.pallas_call(
        functools.partial(_moba_kernel, n_blocks=n_blocks),
        grid=(b, n_blocks),
        in_specs=[
            pl.BlockSpec((1, AT, ATT_W), lambda i, j: (i, j, cb)),
            pl.BlockSpec((1, s, ATT_W), lambda i, j: (i, 0, cb + 1)),
            pl.BlockSpec((1, s, ATT_W), lambda i, j: (i, 0, cb + 2)),
            _const_spec((1, ATT_W)),
        ],
        out_specs=pl.BlockSpec((1, AT, ATT_W), lambda i, j: (i, j, 0)),
        out_shape=jax.ShapeDtypeStruct((b, s, ATT_W), BF16),
        scratch_shapes=[
            pltpu.VMEM((ATT_W, N_AH * AT), BF16),
            pltpu.VMEM((2, s, ATT_W), BF16),
            pltpu.VMEM((n_blocks, ATT_W, AT), BF16),
            pltpu.VMEM((LANES, ATT_W), F32),
            pltpu.VMEM((2 * N_AH, _padded_block_rows(n_blocks), ATT_W), BF16),
            pltpu.VMEM((AT, N_AH * AT), F32),
            pltpu.VMEM((AT, N_AH * AT), F32),
            pltpu.VMEM((ATT_W, AT), F32),
            pltpu.VMEM((2 * N_AH, AT), F32),
        ],
        compiler_params=_params(2),
        name="moba_attn",
    )(proj, proj, proj, norm_w.astype(F32).reshape(1, ATT_W))


OUT_TM = 512
MXU_TILE = 256
_FFN_SPLIT = (FFN_HIDDEN // MXU_TILE + 1) // 2 * MXU_TILE
FFN_CHUNKS = ((0, _FFN_SPLIT), (_FFN_SPLIT, FFN_HIDDEN))


def _out_ffn_kernel(x_ref, ys_ref, yb_ref, ym_ref, wo_ref, n1_ref, n2_ref, wg_ref, wu_ref, wd_ref, n3_ref,
                    o_ref):
    mix = (_dot(ys_ref[0], wo_ref[0:SSD_WIDTH, :])
           + _dot(yb_ref[0], wo_ref[SSD_WIDTH:SSD_WIDTH + SB_WIDTH, :])
           + _dot(ym_ref[0], wo_ref[SSD_WIDTH + SB_WIDTH:, :]))
    x1 = x_ref[0] + _rms(mix) * n1_ref[...]
    h2 = (_rms(x1) * n2_ref[...]).astype(BF16)
    f = None
    for lo, hi in FFN_CHUNKS:
        sl = slice(lo, hi)
        a = (_silu(_dot(h2, wg_ref[:, sl])) * _dot(h2, wu_ref[:, sl])).astype(BF16)
        part = _dot(a, wd_ref[sl, :])
        f = part if f is None else f + part
    o_ref[0] = x1 + _rms(f) * n3_ref[...]


def _out_ffn(x, y_ssd, y_sb, y_moba, w_out, n1, n2, w_gate, w_up, w_down, n3):
    b, s, d = x.shape
    tm = min(OUT_TM, s)
    tile = lambda w: pl.BlockSpec((1, tm, w), lambda i, j: (i, j, 0))
    resident = lambda shape: pl.BlockSpec(shape, lambda i, j: (0, 0), pipeline_mode=pl.Buffered(1))
    vec = lambda v: v.astype(F32).reshape(1, d)
    return pl.pallas_call(
        _out_ffn_kernel,
        grid=(b, s // tm),
        in_specs=[
            tile(d), tile(SSD_WIDTH), tile(SB_WIDTH), tile(MOBA_WIDTH),
            resident((d, d)), _const_spec((1, d)), _const_spec((1, d)),
            resident((d, FFN_HIDDEN)), resident((d, FFN_HIDDEN)), resident((FFN_HIDDEN, d)),
            _const_spec((1, d)),
        ],
        out_specs=tile(d),
        out_shape=jax.ShapeDtypeStruct((b, s, d), F32),
        compiler_params=_params(2),
        name="out_ffn",
    )(x, y_ssd, y_sb, y_moba, w_out, vec(n1), vec(n2), w_gate, w_up, w_down, vec(n3))


def _layer(x, pre_mix_norm, w_in, conv_w, conv_b, dt_bias, a_log, d_skip, ssd_norm, sb_norm, moba_norm,
           w_out, post_mix_norm, pre_ffn_norm, w_gate, w_up, w_down, post_ffn_norm):
    o1 = SSD_WIDTH
    o2 = o1 + CONV_CH
    o3 = o2 + SSD_HEADS
    w_main = jnp.concatenate([w_in[:, o1:o2], w_in[:, :o1], w_in[:, o3:]], axis=1).astype(BF16)
    w_dt = w_in[:, o2:o3].astype(BF16)
    w_dt_pad = jnp.pad(w_dt, ((0, 0), (0, LANES - SSD_HEADS)))
    proj, dt, dtT = _in_proj(x, pre_mix_norm.astype(F32).reshape(1, D_MODEL), w_main, w_dt_pad, w_dt.T)
    y_ssd = _ssd(proj, dt, dtT, conv_w, conv_b, dt_bias, a_log, d_skip, ssd_norm)
    y_sb = _sb_attention(proj, sb_norm)
    y_moba = _moba_attention(proj, moba_norm)
    return _out_ffn(x, y_ssd, y_sb, y_moba, w_out.astype(BF16), post_mix_norm, pre_ffn_norm,
                    w_gate.astype(BF16), w_up.astype(BF16), w_down.astype(BF16), post_ffn_norm)


def kernel(x, pre_mix_norm, w_in, conv_w, conv_b, dt_bias, a_log, d_skip, ssd_norm, sb_norm, moba_norm, w_out,
           post_mix_norm, pre_ffn_norm, w_gate, w_up, w_down, post_ffn_norm):
    depth = w_in.shape[0]
    for l in range(depth):
        x = _layer(x, pre_mix_norm[l], w_in[l], conv_w[l], conv_b[l], dt_bias[l], a_log[l], d_skip[l],
                   ssd_norm[l], sb_norm[l], moba_norm[l], w_out[l], post_mix_norm[l], pre_ffn_norm[l],
                   w_gate[l], w_up[l], w_down[l], post_ffn_norm[l])
    return x
```

```python
import functools
import math

import jax
import jax.numpy as jnp
from jax import lax
from jax.experimental import pallas as pl
from jax.experimental.pallas import tpu as pltpu

F32 = jnp.float32
BF16 = jnp.bfloat16

D_MODEL = 1024
HEAD_DIM = 64
SSD_WIDTH = 512
SSD_HEADS = 8
SSD_GROUPS = 2
SSD_STATE = 128
SSD_CONV = 4
SSD_CHUNK = 128
SB_WIDTH = 256
SB_HEADS = 4
MOBA_WIDTH = 256
MOBA_HEADS = 4
MOBA_BLOCK = 256
MOBA_TOPK = 3
CONV_CH = SSD_WIDTH + 2 * SSD_GROUPS * SSD_STATE
FFN_HIDDEN = 2816
EPS = 1e-6

LANES = 128
NEG_BIG = -1e30

COL_XBC = 0
COL_Z = CONV_CH
COL_SB = COL_Z + SSD_WIDTH
COL_MOBA = COL_SB + 3 * SB_WIDTH
PROJ_COLS = COL_MOBA + 3 * MOBA_WIDTH

VMEM_LIMIT = 56 * 1024 * 1024

_NT = (((1,), (1,)), ((), ()))
_TN = (((0,), (0,)), ((), ()))


def _params(n_axes):
    return pltpu.CompilerParams(dimension_semantics=("arbitrary",) * n_axes,
                                vmem_limit_bytes=VMEM_LIMIT)


def _const_spec(shape):
    zeros = (0,) * len(shape)
    return pl.BlockSpec(shape, lambda *_: zeros)


def _dot(a, b):
    return jnp.dot(a, b, preferred_element_type=F32)


def _dot_nt(a, b):
    return lax.dot_general(a, b, _NT, preferred_element_type=F32)


def _split3(x):
    hi = x.astype(BF16)
    r1 = x - hi.astype(F32)
    mid = r1.astype(BF16)
    lo = (r1 - mid.astype(F32)).astype(BF16)
    return hi, mid, lo


def _dot_exact_rhs(x, m):
    hi, mid, lo = _split3(x)
    return _dot(hi, m) + _dot(mid, m) + _dot(lo, m)


def _dot_exact_lhs(m, x):
    hi, mid, lo = _split3(x)
    return _dot(m, hi) + _dot(m, mid) + _dot(m, lo)


def _softplus(x):
    return jnp.maximum(x, 0.0) + jnp.log(1.0 + jnp.exp(-jnp.abs(x)))


def _silu(x):
    return x / (1.0 + jnp.exp(-x))


def _rms(x):
    return x * lax.rsqrt(jnp.mean(x * x, axis=-1, keepdims=True) + EPS)


IN_TM = 512
IN_CW = 512


def _in_proj_kernel(x_ref, nw_ref, w_ref, wdt_ref, wdtT_ref, proj_ref, dt_ref, dtT_ref):
    x = x_ref[0]
    h = (_rms(x) * nw_ref[...]).astype(BF16)
    for j in range(PROJ_COLS // IN_CW):
        sl = slice(j * IN_CW, (j + 1) * IN_CW)
        proj_ref[0, :, sl] = _dot(h, w_ref[:, sl]).astype(BF16)
    dt_ref[0] = _dot(h, wdt_ref[...])
    dtT_ref[0] = _dot_nt(wdtT_ref[...], h)


def _in_proj(x, norm_w, w_main, w_dt, w_dtT):
    b, s, d = x.shape
    tm = min(IN_TM, s)
    return pl.pallas_call(
        _in_proj_kernel,
        grid=(b, s // tm),
        in_specs=[
            pl.BlockSpec((1, tm, d), lambda i, j: (i, j, 0)),
            _const_spec((1, d)),
            _const_spec((d, PROJ_COLS)),
            _const_spec((d, LANES)),
            _const_spec((SSD_HEADS, d)),
        ],
        out_specs=[
            pl.BlockSpec((1, tm, PROJ_COLS), lambda i, j: (i, j, 0)),
            pl.BlockSpec((1, tm, LANES), lambda i, j: (i, j, 0)),
            pl.BlockSpec((1, SSD_HEADS, tm), lambda i, j: (i, 0, j)),
        ],
        out_shape=[
            jax.ShapeDtypeStruct((b, s, PROJ_COLS), BF16),
            jax.ShapeDtypeStruct((b, s, LANES), F32),
            jax.ShapeDtypeStruct((b, SSD_HEADS, s), F32),
        ],
        compiler_params=_params(2),
        name="in_proj",
    )(x, norm_w, w_main, w_dt, w_dtT)


Q = SSD_CHUNK
TAIL = 16
SSD_STEP_CHUNKS = 4
GW = SSD_WIDTH // SSD_GROUPS
GN = SSD_GROUPS * SSD_STATE


def _ssd_kernel(xbc_ref, z_ref, dt_ref, dtT_ref, cw_ref, cb_ref, dtb_ref, alog_ref, dtbT_ref, alogT_ref,
                dskip_ref, nw_ref, tril_ref, triu_ref, e64_ref, e128_ref, shift_ref,
                y_ref, ext_ref, state_ref):
    c = pl.program_id(1)

    @pl.when(c == 0)
    def _():
        ext_ref[0:TAIL, :] = jnp.zeros((TAIL, CONV_CH), BF16)
        state_ref[...] = jnp.zeros_like(state_ref)

    ext_ref[TAIL:, :] = xbc_ref[0]
    for ci in range(SSD_STEP_CHUNKS):
        _ssd_chunk(ci * Q, z_ref, dt_ref, dtT_ref, cw_ref, cb_ref, dtb_ref, alog_ref, dtbT_ref, alogT_ref,
                   dskip_ref, nw_ref, tril_ref, triu_ref, e64_ref, e128_ref, shift_ref, y_ref, ext_ref,
                   state_ref)
    ext_ref[0:TAIL, :] = ext_ref[SSD_STEP_CHUNKS * Q:, :]


def _ssd_chunk(r0, z_ref, dt_ref, dtT_ref, cw_ref, cb_ref, dtb_ref, alog_ref, dtbT_ref, alogT_ref,
               dskip_ref, nw_ref, tril_ref, triu_ref, e64_ref, e128_ref, shift_ref, y_ref, ext_ref, state_ref):
    rows = slice(r0, r0 + Q)
    taps = _dot(shift_ref[...], ext_ref[r0:r0 + TAIL + Q, :])
    conv = cb_ref[...] + cw_ref[SSD_CONV - 1:SSD_CONV, :] * ext_ref[r0 + TAIL:r0 + TAIL + Q, :].astype(F32)
    for k in range(SSD_CONV - 1):
        conv = conv + cw_ref[k:k + 1, :] * taps[k * Q:(k + 1) * Q, :]
    xa = _silu(conv)
    xs = xa[:, :SSD_WIDTH]
    bm = xa[:, SSD_WIDTH:SSD_WIDTH + GN]
    cm = xa[:, SSD_WIDTH + GN:]

    dtp = _softplus(dt_ref[0, rows, :] + dtb_ref[...])
    adt = dtp * (-jnp.exp(alog_ref[...]))
    acum = _dot_exact_lhs(tril_ref[...], adt)
    dt_exp = _dot(dtp.astype(BF16), e64_ref[...])
    acum_b = _dot_exact_rhs(acum, e128_ref[...])
    lo_half = lax.broadcasted_iota(jnp.int32, (Q, LANES), 1) < HEAD_DIM
    acum_exp = jnp.concatenate(
        [jnp.where(lo_half, acum_b[:, 2 * p * LANES:(2 * p + 1) * LANES],
                   acum_b[:, (2 * p + 1) * LANES:(2 * p + 2) * LANES]) for p in range(SSD_HEADS // 2)], axis=1)
    dtpT = _softplus(dtT_ref[0, :, rows] + dtbT_ref[...])
    adtT = dtpT * (-jnp.exp(alogT_ref[...]))
    acumT = _dot_exact_rhs(adtT, triu_ref[...])

    xdt = xs * dt_exp
    acum_last = acum_exp[Q - 1:Q, :]
    xdd = (xdt * jnp.exp(acum_last - acum_exp)).astype(BF16)
    in_decay = jnp.exp(acum_exp)
    chunk_decay = jnp.exp(acum_last)

    row = lax.broadcasted_iota(jnp.int32, (Q, Q), 0)
    col = lax.broadcasted_iota(jnp.int32, (Q, Q), 1)
    causal = row >= col

    bm16 = bm.astype(BF16)
    cm16 = cm.astype(BF16)
    y_parts = []
    for g in range(SSD_GROUPS):
        bg = bm16[:, g * SSD_STATE:(g + 1) * SSD_STATE]
        cg = cm16[:, g * SSD_STATE:(g + 1) * SSD_STATE]
        cb = _dot_nt(cg, bg)
        gsl = slice(g * GW, (g + 1) * GW)
        st = state_ref[g]
        y_off = _dot(cg, st.astype(BF16)) * in_decay[:, gsl]
        s_new = lax.dot_general(bg, xdd[:, gsl], _TN, preferred_element_type=F32)
        state_ref[g] = st * chunk_decay[:, gsl] + s_new
        for pp in range(2):
            p = g * 2 + pp
            ms = []
            for hh in range(2):
                h = p * 2 + hh
                seg = acum_b[:, h * LANES:(h + 1) * LANES] - acumT[h:h + 1, :]
                lmat = jnp.exp(jnp.where(causal, seg, -jnp.inf))
                ms.append((cb * lmat).astype(BF16))
            mcat = jnp.concatenate(ms, axis=1)
            xp = xdt[:, p * LANES:(p + 1) * LANES]
            rhs = jnp.concatenate([jnp.where(lo_half, xp, 0.0), jnp.where(lo_half, 0.0, xp)],
                                  axis=0).astype(BF16)
            y_parts.append(_dot(mcat, rhs) + y_off[:, pp * LANES:(pp + 1) * LANES])
    y = jnp.concatenate(y_parts, axis=1) + dskip_ref[...] * xs
    y = y * _silu(z_ref[0, rows, :].astype(F32))
    y = jnp.concatenate([_rms(y[:, g * GW:(g + 1) * GW]) for g in range(SSD_GROUPS)], axis=1)
    y_ref[0, rows, :] = (y * nw_ref[...]).astype(BF16)


def _ssd(proj, dt, dtT, conv_w, conv_b, dt_bias, a_log, d_skip, norm_w):
    b, s, _ = proj.shape
    r = SSD_STEP_CHUNKS * Q
    pad8 = lambda v: jnp.pad(v.astype(F32), (0, LANES - SSD_HEADS)).reshape(1, LANES)
    col8 = lambda v: v.astype(F32).reshape(SSD_HEADS, 1)
    ii = jnp.arange(Q)
    tril = (ii[:, None] >= ii[None, :]).astype(BF16)
    triu = (ii[:, None] <= ii[None, :]).astype(BF16)
    hh = jnp.arange(LANES)
    e64 = (hh[:, None] == (jnp.arange(SSD_WIDTH) // HEAD_DIM)[None, :]).astype(BF16)
    e128 = (hh[:, None] == (jnp.arange(SSD_HEADS * LANES) // LANES)[None, :]).astype(BF16)
    dskip = jnp.repeat(d_skip.astype(F32), HEAD_DIM).reshape(1, SSD_WIDTH)
    n_shift = (SSD_CONV - 1) * Q
    tap = jnp.arange(n_shift) // Q
    src = TAIL + (jnp.arange(n_shift) % Q) - (SSD_CONV - 1) + tap
    shift = (src[:, None] == jnp.arange(TAIL + Q)[None, :]).astype(BF16)
    return pl.pallas_call(
        _ssd_kernel,
        grid=(b, s // r),
        in_specs=[
            pl.BlockSpec((1, r, CONV_CH), lambda i, j: (i, j, COL_XBC // CONV_CH)),
            pl.BlockSpec((1, r, SSD_WIDTH), lambda i, j: (i, j, COL_Z // SSD_WIDTH)),
            pl.BlockSpec((1, r, LANES), lambda i, j: (i, j, 0)),
            pl.BlockSpec((1, SSD_HEADS, r), lambda i, j: (i, 0, j)),
            _const_spec((SSD_CONV, CONV_CH)),
            _const_spec((1, CONV_CH)),
            _const_spec((1, LANES)),
            _const_spec((1, LANES)),
            _const_spec((SSD_HEADS, 1)),
            _const_spec((SSD_HEADS, 1)),
            _const_spec((1, SSD_WIDTH)),
            _const_spec((1, SSD_WIDTH)),
            _const_spec((Q, Q)),
            _const_spec((Q, Q)),
            _const_spec((LANES, SSD_WIDTH)),
            _const_spec((LANES, SSD_HEADS * LANES)),
            _const_spec(((SSD_CONV - 1) * Q, TAIL + Q)),
        ],
        out_specs=pl.BlockSpec((1, r, SSD_WIDTH), lambda i, j: (i, j, 0)),
        out_shape=jax.ShapeDtypeStruct((b, s, SSD_WIDTH), BF16),
        scratch_shapes=[
            pltpu.VMEM((TAIL + r, CONV_CH), BF16),
            pltpu.VMEM((SSD_GROUPS, SSD_STATE, GW), F32),
        ],
        compiler_params=_params(2),
        name="ssd",
    )(proj, proj, dt, dtT, conv_w.astype(F32), conv_b.astype(F32).reshape(1, CONV_CH),
      pad8(dt_bias), pad8(a_log), col8(dt_bias), col8(a_log), dskip,
      norm_w.astype(F32).reshape(1, SSD_WIDTH), tril, triu, e64, e128, shift)


AT = 256
N_AH = 4
ATT_W = N_AH * HEAD_DIM
QK_SCALE = 1.0 / math.sqrt(HEAD_DIM)
LOG2E = 1.4426950408889634
SIGN_BIT = -2 ** 31


def _head_masks(rows):
    lane = lax.broadcasted_iota(jnp.int32, (rows, ATT_W), 1)
    return [(lane >= h * HEAD_DIM) & (lane < (h + 1) * HEAD_DIM) for h in range(N_AH)]


def _merge_columns(cols, masks):
    out = jnp.broadcast_to(cols[N_AH - 1], masks[0].shape)
    for h in range(N_AH - 2, -1, -1):
        out = jnp.where(masks[h], cols[h], out)
    return out


def _fill_block_diag(x_ref, bd_ref, n_blocks, masks):
    def body(n, _):
        start = pl.multiple_of(n * AT, AT)
        blk = x_ref[0, pl.ds(start, AT), :]
        for h in range(N_AH):
            bd_ref[n, h * AT:(h + 1) * AT, :] = jnp.where(masks[h], blk, jnp.zeros_like(blk))
        return 0
    lax.fori_loop(0, n_blocks, body, 0)


def _sb_kernel(q_ref, k_ref, v_ref, nw_ref, ut_ref, o_ref, wq_ref, vt_ref, za_ref, zb_ref, acc_ref,
               *, n_blocks):
    qi = pl.program_id(1)

    @pl.when(qi == 0)
    def _():
        def body(n, _):
            start = pl.multiple_of(n * AT, AT)
            vt_ref[n] = v_ref[0, pl.ds(start, AT), :].astype(F32).T.astype(BF16)
            return 0
        lax.fori_loop(0, n_blocks, body, 0)

    qt = (q_ref[0].astype(F32) * (QK_SCALE * LOG2E)).T
    lane_row = lax.broadcasted_iota(jnp.int32, (ATT_W, AT), 0)
    heads = [slice(h * AT, (h + 1) * AT) for h in range(N_AH)]
    head_rows = [slice(h * HEAD_DIM, (h + 1) * HEAD_DIM) for h in range(N_AH)]
    for h in range(N_AH):
        own = (lane_row >= h * HEAD_DIM) & (lane_row < (h + 1) * HEAD_DIM)
        wq_ref[:, heads[h]] = jnp.where(own, qt, 0.0).astype(BF16)
    key = lax.broadcasted_iota(jnp.int32, (AT, N_AH * AT), 0)
    qry = lax.broadcasted_iota(jnp.int32, (AT, N_AH * AT), 1) & (AT - 1)
    strict = key < qry

    def scores(kb, z_ref):
        start = pl.multiple_of(kb * AT, AT)
        z_ref[...] = _dot(k_ref[0, pl.ds(start, AT), :], wq_ref[...])

    def step(kb, z_ref, carry, diag=False):
        z = z_ref[...]
        neg_abs = pltpu.bitcast(pltpu.bitcast(z, jnp.int32) | SIGN_BIT, F32)
        sp = jnp.maximum(z, 0.0) + jnp.log(1.0 + jnp.exp2(neg_abs)) * LOG2E
        if diag:
            sp = jnp.where(strict, sp, 0.0)
        cs = _dot(ut_ref[...], sp.astype(BF16))
        logw = z - sp - cs
        if diag:
            logw = jnp.where(strict, logw, NEG_BIG)
        w = jnp.exp2(logw).astype(BF16)
        later = jnp.exp2(-carry)
        pv = jnp.concatenate(
            [_dot(vt_ref[kb, head_rows[h], :], w[:, heads[h]]) * later[:, heads[h]] for h in range(N_AH)],
            axis=0)
        if diag:
            acc_ref[...] = pv
        else:
            acc_ref[...] += pv
        return carry + cs[0:1, :] + sp[0:1, :]

    scores(qi, zb_ref)
    scores(jnp.maximum(qi - 1, 0), za_ref)
    carry = step(qi, zb_ref, jnp.zeros((1, N_AH * AT), F32), diag=True)

    def two_blocks(j, carry):
        kb = qi - 1 - 2 * j
        scores(kb - 1, zb_ref)
        carry = step(kb, za_ref, carry)
        scores(jnp.maximum(kb - 2, 0), za_ref)
        return step(kb - 1, zb_ref, carry)

    carry = lax.fori_loop(0, qi // 2, two_blocks, carry)

    @pl.when(qi % 2 == 1)
    def _():
        step(0, za_ref, carry)

    o_ref[0] = (_rms(acc_ref[...].T) * nw_ref[...]).astype(BF16)


def _sb_attention(proj, norm_w):
    b, s, _ = proj.shape
    n_blocks = s // AT
    ii = jnp.arange(AT)
    ut = (ii[:, None] < ii[None, :]).astype(BF16)
    cb = COL_SB // ATT_W
    return pl.pallas_call(
        functools.partial(_sb_kernel, n_blocks=n_blocks),
        grid=(b, n_blocks),
        in_specs=[
            pl.BlockSpec((1, AT, ATT_W), lambda i, j: (i, j, cb)),
            pl.BlockSpec((1, s, ATT_W), lambda i, j: (i, 0, cb + 1)),
            pl.BlockSpec((1, s, ATT_W), lambda i, j: (i, 0, cb + 2)),
            _const_spec((1, ATT_W)),
            _const_spec((AT, AT)),
        ],
        out_specs=pl.BlockSpec((1, AT, ATT_W), lambda i, j: (i, j, 0)),
        out_shape=jax.ShapeDtypeStruct((b, s, ATT_W), BF16),
        scratch_shapes=[
            pltpu.VMEM((ATT_W, N_AH * AT), BF16),
            pltpu.VMEM((n_blocks, ATT_W, AT), BF16),
            pltpu.VMEM((AT, N_AH * AT), F32),
            pltpu.VMEM((AT, N_AH * AT), F32),
            pltpu.VMEM((ATT_W, AT), F32),
        ],
        compiler_params=_params(2),
        name="sb_attn",
    )(proj, proj, proj, norm_w.astype(F32).reshape(1, ATT_W), ut)


MOBA_SLOPES = tuple(2.0 ** (-8.0 * i / MOBA_HEADS) for i in range(1, MOBA_HEADS + 1))
POS_LANE = LANES - 1
BF16_SUBLANES = 16


def _padded_block_rows(n_blocks):
    return -(-n_blocks // BF16_SUBLANES) * BF16_SUBLANES


def _moba_kernel(q_ref, k_ref, v_ref, nw_ref, o_ref, w_ref, kaug_ref, vt_ref, kmean_ref, kmh_ref, sa_ref,
                 sb_ref, acc_ref, l_ref, *, n_blocks):
    qi = pl.program_id(1)
    nbp = _padded_block_rows(n_blocks)

    @pl.when(qi == 0)
    def _():
        kmean_ref[...] = jnp.zeros_like(kmean_ref)
        wl = lax.broadcasted_iota(jnp.int32, (AT, LANES), 1)
        wr = lax.broadcasted_iota(jnp.int32, (AT, LANES), 0)

        def body(n, _):
            start = pl.multiple_of(n * AT, AT)
            kblk = k_ref[0, pl.ds(start, AT), :]
            kmean_ref[pl.ds(n, 1), :] = jnp.mean(kblk.astype(F32), axis=0, keepdims=True)
            window = jnp.where(wl == POS_LANE, wr, (wl == n).astype(jnp.int32)).astype(F32).astype(BF16)
            kaug_ref[0, pl.ds(start, AT), :] = jnp.concatenate([kblk[:, :LANES], window], axis=1)
            kaug_ref[1, pl.ds(start, AT), :] = jnp.concatenate([window, kblk[:, LANES:]], axis=1)
            vt_ref[n] = v_ref[0, pl.ds(start, AT), :].astype(F32).T.astype(BF16)
            return 0
        lax.fori_loop(0, n_blocks, body, 0)
        km = kmean_ref[0:nbp, :]
        masks = _head_masks(nbp)
        for h in range(N_AH):
            kmh = jnp.where(masks[h], km, 0.0)
            hi = kmh.astype(BF16)
            kmh_ref[h] = hi
            kmh_ref[N_AH + h] = (kmh - hi.astype(F32)).astype(BF16)

    qt = (q_ref[0].astype(F32) * QK_SCALE).T
    qt16 = qt.astype(BF16)
    row = lax.broadcasted_iota(jnp.int32, (LANES, AT), 0)
    blk = lax.broadcasted_iota(jnp.int32, (nbp, AT), 0)
    blk_f = blk.astype(F32)
    tail_row = lax.broadcasted_iota(jnp.int32, (LANES - nbp, AT), 0) + nbp
    for h in range(N_AH):
        gate = _dot(kmh_ref[h], qt16) + _dot(kmh_ref[N_AH + h], qt16)
        gate = jnp.where(blk < qi, gate, -jnp.inf)
        chosen = jnp.zeros((nbp, AT), F32)
        for _ in range(MOBA_TOPK):
            m = jnp.max(gate, axis=0, keepdims=True)
            hit = jnp.where(gate == m, blk_f, float(LANES))
            first = jnp.min(hit, axis=0, keepdims=True)
            pick = (blk_f == first) & (m > -jnp.inf)
            chosen = jnp.where(pick, 1.0, chosen)
            gate = jnp.where(pick, -jnp.inf, gate)
        bias = jnp.where((chosen > 0.5) | (blk == qi), 0.0, NEG_BIG)
        tail = jnp.where(tail_row == POS_LANE, MOBA_SLOPES[h], 0.0)
        half, sub = divmod(h, 2)
        q_half = qt[half * LANES:(half + 1) * LANES, :]
        q_half = jnp.where((row >= sub * HEAD_DIM) & (row < (sub + 1) * HEAD_DIM), q_half, 0.0)
        cols = slice(h * AT, (h + 1) * AT)
        q0, b0 = (0, LANES) if half == 0 else (LANES, 0)
        w_ref[q0:q0 + LANES, cols] = q_half.astype(BF16)
        w_ref[b0:b0 + nbp, cols] = bias.astype(BF16)
        w_ref[b0 + nbp:b0 + LANES, cols] = tail.astype(BF16)

    key = lax.broadcasted_iota(jnp.int32, (AT, AT), 0)
    qry = lax.broadcasted_iota(jnp.int32, (AT, AT), 1)
    causal = key <= qry
    heads = [slice(h * AT, (h + 1) * AT) for h in range(N_AH)]
    head_rows = [slice(h * HEAD_DIM, (h + 1) * HEAD_DIM) for h in range(N_AH)]

    def scores(kb, s_ref):
        start = pl.multiple_of(kb * AT, AT)
        for half in range(2):
            cols = slice(2 * half * AT, 2 * (half + 1) * AT)
            s_ref[:, cols] = _dot(kaug_ref[half, pl.ds(start, AT), :], w_ref[:, cols])

    def update(kb, s_ref, stats, own=False):
        ms, ls = stats
        dist = ((kb - qi) * AT).astype(F32)
        nm, nl, scale, pv = [], [], [], []
        for h in range(N_AH):
            s = s_ref[:, heads[h]]
            if own:
                s = jnp.where(causal, s, NEG_BIG)
                m_new = jnp.max(s, axis=0, keepdims=True)
                p = jnp.exp(s - m_new)
                l_new = jnp.sum(p, axis=0, keepdims=True)
            else:
                c = MOBA_SLOPES[h] * dist
                m_new = jnp.maximum(ms[h], jnp.max(s, axis=0, keepdims=True) + c)
                alpha = jnp.exp(ms[h] - m_new)
                p = jnp.exp(s - (m_new - c))
                l_new = alpha * ls[h] + jnp.sum(p, axis=0, keepdims=True)
                scale.append(jnp.broadcast_to(alpha, (HEAD_DIM, AT)))
            nm.append(m_new)
            nl.append(l_new)
            pv.append(_dot(vt_ref[kb, head_rows[h], :], p.astype(BF16)))
        pv = jnp.concatenate(pv, axis=0)
        if own:
            acc_ref[...] = pv
        else:
            acc_ref[...] = acc_ref[...] * jnp.concatenate(scale, axis=0) + pv
        return tuple(nm), tuple(nl)

    scores(qi, sb_ref)
    scores(0, sa_ref)
    stats = update(qi, sb_ref, (None, None), own=True)

    def two_blocks(j, stats):
        kb = 2 * j
        scores(kb + 1, sb_ref)
        stats = update(kb, sa_ref, stats)
        scores(jnp.minimum(kb + 2, qi), sa_ref)
        return update(kb + 1, sb_ref, stats)

    stats = lax.fori_loop(0, qi // 2, two_blocks, stats)
    for h in range(N_AH):
        l_ref[h:h + 1, :] = stats[1][h]

    @pl.when(qi % 2 == 1)
    def _():
        _, ls_odd = update(qi - 1, sa_ref, stats)
        for h in range(N_AH):
            l_ref[h:h + 1, :] = ls_odd[h]

    yt = jnp.concatenate([acc_ref[head_rows[h], :] / l_ref[h:h + 1, :] for h in range(N_AH)], axis=0)
    o_ref[0] = (_rms(yt.T) * nw_ref[...]).astype(BF16)


def _moba_attention(proj, norm_w):
    b, s, _ = proj.shape
    n_blocks = s // AT
    assert s % AT == 0 and n_blocks < POS_LANE
    cb = COL_MOBA // ATT_W
    return pl.pallas_call(
        functools.partial(_moba_kernel, n_blocks=n_blocks),
        grid=(b, n_blocks),
        in_specs=[
            pl.BlockSpec((1, AT, ATT_W), lambda i, j: (i, j, cb)),
            pl.BlockSpec((1, s, ATT_W), lambda i, j: (i, 0, cb + 1)),
            pl.BlockSpec((1, s, ATT_W), lambda i, j: (i, 0, cb + 2)),
            _const_spec((1, ATT_W)),
        ],
        out_specs=pl.BlockSpec((1, AT, ATT_W), lambda i, j: (i, j, 0)),
        out_shape=jax.ShapeDtypeStruct((b, s, ATT_W), BF16),
        scratch_shapes=[
            pltpu.VMEM((ATT_W, N_AH * AT), BF16),
            pltpu.VMEM((2, s, ATT_W), BF16),
            pltpu.VMEM((n_blocks, ATT_W, AT), BF16),
            pltpu.VMEM((LANES, ATT_W), F32),
            pltpu.VMEM((2 * N_AH, _padded_block_rows(n_blocks), ATT_W), BF16),
            pltpu.VMEM((AT, N_AH * AT), F32),
            pltpu.VMEM((AT, N_AH * AT), F32),
            pltpu.VMEM((ATT_W, AT), F32),
            pltpu.VMEM((2 * N_AH, AT), F32),
        ],
        compiler_params=_params(2),
        name="moba_attn",
    )(proj, proj, proj, norm_w.astype(F32).reshape(1, ATT_W))


OUT_TM = 512
MXU_TILE = 256
_FFN_SPLIT = (FFN_HIDDEN // MXU_TILE + 1) // 2 * MXU_TILE
FFN_CHUNKS = ((0, _FFN_SPLIT), (_FFN_SPLIT, FFN_HIDDEN))


def _out_ffn_kernel(x_ref, ys_ref, yb_ref, ym_ref, wo_ref, n1_ref, n2_ref, wg_ref, wu_ref, wd_ref, n3_ref,
                    o_ref):
    mix = (_dot(ys_ref[0], wo_ref[0:SSD_WIDTH, :])
           + _dot(yb_ref[0], wo_ref[SSD_WIDTH:SSD_WIDTH + SB_WIDTH, :])
           + _dot(ym_ref[0], wo_ref[SSD_WIDTH + SB_WIDTH:, :]))
    x1 = x_ref[0] + _rms(mix) * n1_ref[...]
    h2 = (_rms(x1) * n2_ref[...]).astype(BF16)
    f = None
    for lo, hi in FFN_CHUNKS:
        sl = slice(lo, hi)
        a = (_silu(_dot(h2, wg_ref[:, sl])) * _dot(h2, wu_ref[:, sl])).astype(BF16)
        part = _dot(a, wd_ref[sl, :])
        f = part if f is None else f + part
    o_ref[0] = x1 + _rms(f) * n3_ref[...]


def _out_ffn(x, y_ssd, y_sb, y_moba, w_out, n1, n2, w_gate, w_up, w_down, n3):
    b, s, d = x.shape
    tm = min(OUT_TM, s)
    tile = lambda w: pl.BlockSpec((1, tm, w), lambda i, j: (i, j, 0))
    resident = lambda shape: pl.BlockSpec(shape, lambda i, j: (0, 0), pipeline_mode=pl.Buffered(1))
    vec = lambda v: v.astype(F32).reshape(1, d)
    return pl.pallas_call(
        _out_ffn_kernel,
        grid=(b, s // tm),
        in_specs=[
            tile(d), tile(SSD_WIDTH), tile(SB_WIDTH), tile(MOBA_WIDTH),
            resident((d, d)), _const_spec((1, d)), _const_spec((1, d)),
            resident((d, FFN_HIDDEN)), resident((d, FFN_HIDDEN)), resident((FFN_HIDDEN, d)),
            _const_spec((1, d)),
        ],
        out_specs=tile(d),
        out_shape=jax.ShapeDtypeStruct((b, s, d), F32),
        compiler_params=_params(2),
        name="out_ffn",
    )(x, y_ssd, y_sb, y_moba, w_out, vec(n1), vec(n2), w_gate, w_up, w_down, vec(n3))


def _layer(x, pre_mix_norm, w_in, conv_w, conv_b, dt_bias, a_log, d_skip, ssd_norm, sb_norm, moba_norm,
           w_out, post_mix_norm, pre_ffn_norm, w_gate, w_up, w_down, post_ffn_norm):
    o1 = SSD_WIDTH
    o2 = o1 + CONV_CH
    o3 = o2 + SSD_HEADS
    w_main = jnp.concatenate([w_in[:, o1:o2], w_in[:, :o1], w_in[:, o3:]], axis=1).astype(BF16)
    w_dt = w_in[:, o2:o3].astype(BF16)
    w_dt_pad = jnp.pad(w_dt, ((0, 0), (0, LANES - SSD_HEADS)))
    proj, dt, dtT = _in_proj(x, pre_mix_norm.astype(F32).reshape(1, D_MODEL), w_main, w_dt_pad, w_dt.T)
    y_ssd = _ssd(proj, dt, dtT, conv_w, conv_b, dt_bias, a_log, d_skip, ssd_norm)
    y_sb = _sb_attention(proj, sb_norm)
    y_moba = _moba_attention(proj, moba_norm)
    return _out_ffn(x, y_ssd, y_sb, y_moba, w_out.astype(BF16), post_mix_norm, pre_ffn_norm,
                    w_gate.astype(BF16), w_up.astype(BF16), w_down.astype(BF16), post_ffn_norm)


def kernel(x, pre_mix_norm, w_in, conv_w, conv_b, dt_bias, a_log, d_skip, ssd_norm, sb_norm, moba_norm, w_out,
           post_mix_norm, pre_ffn_norm, w_gate, w_up, w_down, post_ffn_norm):
    depth = w_in.shape[0]
    for l in range(depth):
        x = _layer(x, pre_mix_norm[l], w_in[l], conv_w[l], conv_b[l], dt_bias[l], a_log[l], d_skip[l],
                   ssd_norm[l], sb_norm[l], moba_norm[l], w_out[l], post_mix_norm[l], pre_ffn_norm[l],
                   w_gate[l], w_up[l], w_down[l], post_ffn_norm[l])
    return x
```

```python
import functools
import math

import jax
import jax.numpy as jnp
from jax import lax
from jax.experimental import pallas as pl
from jax.experimental.pallas import tpu as pltpu

F32 = jnp.float32
BF16 = jnp.bfloat16

D_MODEL = 1024
HEAD_DIM = 64
SSD_WIDTH = 512
SSD_HEADS = 8
SSD_GROUPS = 2
SSD_STATE = 128
SSD_CONV = 4
SSD_CHUNK = 128
SB_WIDTH = 256
SB_HEADS = 4
MOBA_WIDTH = 256
MOBA_HEADS = 4
MOBA_BLOCK = 256
MOBA_TOPK = 3
CONV_CH = SSD_WIDTH + 2 * SSD_GROUPS * SSD_STATE
FFN_HIDDEN = 2816
EPS = 1e-6

LANES = 128
BF16_SUBLANES = 16
NEG_BIG = -1e30

COL_XBC = 0
COL_Z = CONV_CH
COL_SB = COL_Z + SSD_WIDTH
COL_MOBA = COL_SB + 3 * SB_WIDTH
PROJ_COLS = COL_MOBA + 3 * MOBA_WIDTH

VMEM_LIMIT = 56 * 1024 * 1024

_NT = (((1,), (1,)), ((), ()))
_TN = (((0,), (0,)), ((), ()))


def _params(n_axes):
    return pltpu.CompilerParams(dimension_semantics=("arbitrary",) * n_axes,
                                vmem_limit_bytes=VMEM_LIMIT)


def _const_spec(shape):
    zeros = (0,) * len(shape)
    return pl.BlockSpec(shape, lambda *_: zeros)


def _dot(a, b):
    return jnp.dot(a, b, preferred_element_type=F32)


def _dot_nt(a, b):
    return lax.dot_general(a, b, _NT, preferred_element_type=F32)


def _dot_tn(a, b):
    return lax.dot_general(a, b, _TN, preferred_element_type=F32)


def _split3(x):
    hi = x.astype(BF16)
    r1 = x - hi.astype(F32)
    mid = r1.astype(BF16)
    lo = (r1 - mid.astype(F32)).astype(BF16)
    return hi, mid, lo


def _dot_exact_rhs(x, m):
    hi, mid, lo = _split3(x)
    return _dot(hi, m) + _dot(mid, m) + _dot(lo, m)


def _softplus(x):
    return jnp.maximum(x, 0.0) + jnp.log(1.0 + jnp.exp(-jnp.abs(x)))


def _silu(x):
    return x / (1.0 + jnp.exp(-x))


def _rms(x):
    return x * lax.rsqrt(jnp.mean(x * x, axis=-1, keepdims=True) + EPS)


IN_TM = 1024
IN_CW = 512


def _in_proj_kernel(x_ref, nw_ref, w_ref, wdtT_ref, proj_ref, dtT_ref):
    x = x_ref[0]
    h = (_rms(x) * nw_ref[...]).astype(BF16)
    for j in range(PROJ_COLS // IN_CW):
        sl = slice(j * IN_CW, (j + 1) * IN_CW)
        proj_ref[0, :, sl] = _dot(h, w_ref[:, sl]).astype(BF16)
    dtT_ref[0] = _dot_nt(wdtT_ref[...], h)


def _in_proj(x, norm_w, w_main, w_dtT):
    b, s, d = x.shape
    tm = min(IN_TM, s)
    return pl.pallas_call(
        _in_proj_kernel,
        grid=(b, s // tm),
        in_specs=[
            pl.BlockSpec((1, tm, d), lambda i, j: (i, j, 0)),
            _const_spec((1, d)),
            _const_spec((d, PROJ_COLS)),
            _const_spec((SSD_HEADS, d)),
        ],
        out_specs=[
            pl.BlockSpec((1, tm, PROJ_COLS), lambda i, j: (i, j, 0)),
            pl.BlockSpec((1, SSD_HEADS, tm), lambda i, j: (i, 0, j)),
        ],
        out_shape=[
            jax.ShapeDtypeStruct((b, s, PROJ_COLS), BF16),
            jax.ShapeDtypeStruct((b, SSD_HEADS, s), F32),
        ],
        compiler_params=_params(2),
        name="in_proj",
    )(x, norm_w, w_main, w_dtT)


Q = SSD_CHUNK
TAIL = 16
SSD_STEP_CHUNKS = 4
GW = SSD_WIDTH // SSD_GROUPS
GN = SSD_GROUPS * SSD_STATE


def _ssd_kernel(xbc_ref, z_ref, dtT_ref, cw_ref, cb_ref, dtbT_ref, alogT_ref,
                dskip_ref, nw_ref, triu_ref, e64_ref, e128_ref, shift_ref,
                y_ref, ext_ref, state_ref):
    c = pl.program_id(1)

    @pl.when(c == 0)
    def _():
        ext_ref[0:TAIL, :] = jnp.zeros((TAIL, CONV_CH), BF16)
        state_ref[...] = jnp.zeros_like(state_ref)

    ext_ref[TAIL:, :] = xbc_ref[0]
    for ci in range(SSD_STEP_CHUNKS):
        _ssd_chunk(ci * Q, z_ref, dtT_ref, cw_ref, cb_ref, dtbT_ref, alogT_ref,
                   dskip_ref, nw_ref, triu_ref, e64_ref, e128_ref, shift_ref, y_ref, ext_ref,
                   state_ref)
    ext_ref[0:TAIL, :] = ext_ref[SSD_STEP_CHUNKS * Q:, :]


def _ssd_chunk(r0, z_ref, dtT_ref, cw_ref, cb_ref, dtbT_ref, alogT_ref,
               dskip_ref, nw_ref, triu_ref, e64_ref, e128_ref, shift_ref, y_ref, ext_ref, state_ref):
    rows = slice(r0, r0 + Q)
    taps = _dot(shift_ref[...], ext_ref[r0:r0 + TAIL + Q, :])
    conv = cb_ref[...] + cw_ref[SSD_CONV - 1:SSD_CONV, :] * ext_ref[r0 + TAIL:r0 + TAIL + Q, :].astype(F32)
    for k in range(SSD_CONV - 1):
        conv = conv + cw_ref[k:k + 1, :] * taps[k * Q:(k + 1) * Q, :]
    xa = _silu(conv)
    xs = xa[:, :SSD_WIDTH]
    bm = xa[:, SSD_WIDTH:SSD_WIDTH + GN]
    cm = xa[:, SSD_WIDTH + GN:]

    dtpT = _softplus(dtT_ref[0, :, rows] + dtbT_ref[...])
    adtT = dtpT * (-jnp.exp(alogT_ref[...]))
    acumT = _dot_exact_rhs(adtT, triu_ref[...])
    pad = jnp.zeros((BF16_SUBLANES - SSD_HEADS, Q), F32)
    dt_exp = _dot_tn(jnp.concatenate([dtpT, pad], axis=0).astype(BF16), e64_ref[...])
    acum_b = sum(_dot_tn(piece, e128_ref[...])
                 for piece in _split3(jnp.concatenate([acumT, pad], axis=0)))
    lo_half = lax.broadcasted_iota(jnp.int32, (Q, LANES), 1) < HEAD_DIM
    acum_exp = jnp.concatenate(
        [jnp.where(lo_half, acum_b[:, 2 * p * LANES:(2 * p + 1) * LANES],
                   acum_b[:, (2 * p + 1) * LANES:(2 * p + 2) * LANES]) for p in range(SSD_HEADS // 2)], axis=1)

    xdt = xs * dt_exp
    acum_last = acum_exp[Q - 1:Q, :]
    xdd = (xdt * jnp.exp(acum_last - acum_exp)).astype(BF16)
    in_decay = jnp.exp(acum_exp)
    chunk_decay = jnp.exp(acum_last)

    row = lax.broadcasted_iota(jnp.int32, (Q, Q), 0)
    col = lax.broadcasted_iota(jnp.int32, (Q, Q), 1)
    causal = row >= col

    bm16 = bm.astype(BF16)
    cm16 = cm.astype(BF16)
    y_parts = []
    for g in range(SSD_GROUPS):
        bg = bm16[:, g * SSD_STATE:(g + 1) * SSD_STATE]
        cg = cm16[:, g * SSD_STATE:(g + 1) * SSD_STATE]
        cb = _dot_nt(cg, bg)
        gsl = slice(g * GW, (g + 1) * GW)
        st = state_ref[g]
        y_off = _dot(cg, st.astype(BF16)) * in_decay[:, gsl]
        s_new = lax.dot_general(bg, xdd[:, gsl], _TN, preferred_element_type=F32)
        state_ref[g] = st * chunk_decay[:, gsl] + s_new
        for pp in range(2):
            p = g * 2 + pp
            ms = []
            for hh in range(2):
                h = p * 2 + hh
                seg = acum_b[:, h * LANES:(h + 1) * LANES] - acumT[h:h + 1, :]
                lmat = jnp.exp(jnp.where(causal, seg, -jnp.inf))
                ms.append((cb * lmat).astype(BF16))
            mcat = jnp.concatenate(ms, axis=1)
            xp = xdt[:, p * LANES:(p + 1) * LANES]
            rhs = jnp.concatenate([jnp.where(lo_half, xp, 0.0), jnp.where(lo_half, 0.0, xp)],
                                  axis=0).astype(BF16)
            y_parts.append(_dot(mcat, rhs) + y_off[:, pp * LANES:(pp + 1) * LANES])
    y = jnp.concatenate(y_parts, axis=1) + dskip_ref[...] * xs
    y = y * _silu(z_ref[0, rows, :].astype(F32))
    y = jnp.concatenate([_rms(y[:, g * GW:(g + 1) * GW]) for g in range(SSD_GROUPS)], axis=1)
    y_ref[0, rows, :] = (y * nw_ref[...]).astype(BF16)


def _ssd(proj, dtT, conv_w, conv_b, dt_bias, a_log, d_skip, norm_w):
    b, s, _ = proj.shape
    r = SSD_STEP_CHUNKS * Q
    col8 = lambda v: v.astype(F32).reshape(SSD_HEADS, 1)
    ii = jnp.arange(Q)
    triu = (ii[:, None] <= ii[None, :]).astype(BF16)
    hh = jnp.arange(BF16_SUBLANES)
    e64 = (hh[:, None] == (jnp.arange(SSD_WIDTH) // HEAD_DIM)[None, :]).astype(BF16)
    e128 = (hh[:, None] == (jnp.arange(SSD_HEADS * LANES) // LANES)[None, :]).astype(BF16)
    dskip = jnp.repeat(d_skip.astype(F32), HEAD_DIM).reshape(1, SSD_WIDTH)
    n_shift = (SSD_CONV - 1) * Q
    tap = jnp.arange(n_shift) // Q
    src = TAIL + (jnp.arange(n_shift) % Q) - (SSD_CONV - 1) + tap
    shift = (src[:, None] == jnp.arange(TAIL + Q)[None, :]).astype(BF16)
    return pl.pallas_call(
        _ssd_kernel,
        grid=(b, s // r),
        in_specs=[
            pl.BlockSpec((1, r, CONV_CH), lambda i, j: (i, j, COL_XBC // CONV_CH)),
            pl.BlockSpec((1, r, SSD_WIDTH), lambda i, j: (i, j, COL_Z // SSD_WIDTH)),
            pl.BlockSpec((1, SSD_HEADS, r), lambda i, j: (i, 0, j)),
            _const_spec((SSD_CONV, CONV_CH)),
            _const_spec((1, CONV_CH)),
            _const_spec((SSD_HEADS, 1)),
            _const_spec((SSD_HEADS, 1)),
            _const_spec((1, SSD_WIDTH)),
            _const_spec((1, SSD_WIDTH)),
            _const_spec((Q, Q)),
            _const_spec((BF16_SUBLANES, SSD_WIDTH)),
            _const_spec((BF16_SUBLANES, SSD_HEADS * LANES)),
            _const_spec(((SSD_CONV - 1) * Q, TAIL + Q)),
        ],
        out_specs=pl.BlockSpec((1, r, SSD_WIDTH), lambda i, j: (i, j, 0)),
        out_shape=jax.ShapeDtypeStruct((b, s, SSD_WIDTH), BF16),
        scratch_shapes=[
            pltpu.VMEM((TAIL + r, CONV_CH), BF16),
            pltpu.VMEM((SSD_GROUPS, SSD_STATE, GW), F32),
        ],
        compiler_params=_params(2),
        name="ssd",
    )(proj, proj, dtT, conv_w.astype(F32), conv_b.astype(F32).reshape(1, CONV_CH),
      col8(dt_bias), col8(a_log), dskip,
      norm_w.astype(F32).reshape(1, SSD_WIDTH), triu, e64, e128, shift)


AT = 256
N_AH = 4
ATT_W = N_AH * HEAD_DIM
QK_SCALE = 1.0 / math.sqrt(HEAD_DIM)
LOG2E = 1.4426950408889634
SIGN_BIT = -2 ** 31


def _head_masks(rows):
    lane = lax.broadcasted_iota(jnp.int32, (rows, ATT_W), 1)
    return [(lane >= h * HEAD_DIM) & (lane < (h + 1) * HEAD_DIM) for h in range(N_AH)]


def _sb_kernel(q_ref, k_ref, v_ref, nw_ref, ut_ref, o_ref, wq_ref, vt_ref, za_ref, zb_ref, acc_ref,
               *, n_blocks):
    qi = pl.program_id(1)

    @pl.when(qi == 0)
    def _():
        def body(n, _):
            start = pl.multiple_of(n * AT, AT)
            vt_ref[n] = v_ref[0, pl.ds(start, AT), :].astype(F32).T.astype(BF16)
            return 0
        lax.fori_loop(0, n_blocks, body, 0)

    qt = (q_ref[0].astype(F32) * (QK_SCALE * LOG2E)).T
    lane_row = lax.broadcasted_iota(jnp.int32, (ATT_W, AT), 0)
    heads = [slice(h * AT, (h + 1) * AT) for h in range(N_AH)]
    head_rows = [slice(h * HEAD_DIM, (h + 1) * HEAD_DIM) for h in range(N_AH)]
    for h in range(N_AH):
        own = (lane_row >= h * HEAD_DIM) & (lane_row < (h + 1) * HEAD_DIM)
        wq_ref[:, heads[h]] = jnp.where(own, qt, 0.0).astype(BF16)
    key = lax.broadcasted_iota(jnp.int32, (AT, N_AH * AT), 0)
    qry = lax.broadcasted_iota(jnp.int32, (AT, N_AH * AT), 1) & (AT - 1)
    strict = key < qry

    def scores(kb, z_ref):
        start = pl.multiple_of(kb * AT, AT)
        z_ref[...] = _dot(k_ref[0, pl.ds(start, AT), :], wq_ref[...])

    def step(kb, z_ref, carry, diag=False):
        z = z_ref[...]
        neg_abs = pltpu.bitcast(pltpu.bitcast(z, jnp.int32) | SIGN_BIT, F32)
        sp = jnp.maximum(z, 0.0) + jnp.log(1.0 + jnp.exp2(neg_abs)) * LOG2E
        if diag:
            sp = jnp.where(strict, sp, 0.0)
        cs = _dot(ut_ref[...], sp.astype(BF16))
        logw = z - sp - cs
        if diag:
            logw = jnp.where(strict, logw, NEG_BIG)
        w = jnp.exp2(logw).astype(BF16)
        later = jnp.exp2(-carry)
        pv = jnp.concatenate(
            [_dot(vt_ref[kb, head_rows[h], :], w[:, heads[h]]) * later[:, heads[h]] for h in range(N_AH)],
            axis=0)
        if diag:
            acc_ref[...] = pv
        else:
            acc_ref[...] += pv
        return carry + cs[0:1, :] + sp[0:1, :]

    scores(qi, zb_ref)
    scores(jnp.maximum(qi - 1, 0), za_ref)
    carry = step(qi, zb_ref, jnp.zeros((1, N_AH * AT), F32), diag=True)

    def two_blocks(j, carry):
        kb = qi - 1 - 2 * j
        scores(kb - 1, zb_ref)
        carry = step(kb, za_ref, carry)
        scores(jnp.maximum(kb - 2, 0), za_ref)
        return step(kb - 1, zb_ref, carry)

    carry = lax.fori_loop(0, qi // 2, two_blocks, carry)

    @pl.when(qi % 2 == 1)
    def _():
        step(0, za_ref, carry)

    o_ref[0] = (_rms(acc_ref[...].T) * nw_ref[...]).astype(BF16)


def _sb_attention(proj, norm_w):
    b, s, _ = proj.shape
    n_blocks = s // AT
    ii = jnp.arange(AT)
    ut = (ii[:, None] < ii[None, :]).astype(BF16)
    cb = COL_SB // ATT_W
    return pl.pallas_call(
        functools.partial(_sb_kernel, n_blocks=n_blocks),
        grid=(b, n_blocks),
        in_specs=[
            pl.BlockSpec((1, AT, ATT_W), lambda i, j: (i, j, cb)),
            pl.BlockSpec((1, s, ATT_W), lambda i, j: (i, 0, cb + 1)),
            pl.BlockSpec((1, s, ATT_W), lambda i, j: (i, 0, cb + 2)),
            _const_spec((1, ATT_W)),
            _const_spec((AT, AT)),
        ],
        out_specs=pl.BlockSpec((1, AT, ATT_W), lambda i, j: (i, j, 0)),
        out_shape=jax.ShapeDtypeStruct((b, s, ATT_W), BF16),
        scratch_shapes=[
            pltpu.VMEM((ATT_W, N_AH * AT), BF16),
            pltpu.VMEM((n_blocks, ATT_W, AT), BF16),
            pltpu.VMEM((AT, N_AH * AT), F32),
            pltpu.VMEM((AT, N_AH * AT), F32),
            pltpu.VMEM((ATT_W, AT), F32),
        ],
        compiler_params=_params(2),
        name="sb_attn",
    )(proj, proj, proj, norm_w.astype(F32).reshape(1, ATT_W), ut)


MOBA_SLOPES = tuple(2.0 ** (-8.0 * i / MOBA_HEADS) for i in range(1, MOBA_HEADS + 1))
POS_LANE = LANES - 1


def _padded_block_rows(n_blocks):
    return -(-n_blocks // BF16_SUBLANES) * BF16_SUBLANES


def _moba_kernel(q_ref, k_ref, v_ref, nw_ref, o_ref, w_ref, kaug_ref, vt_ref, kmean_ref, kmh_ref, sa_ref,
                 sb_ref, acc_ref, l_ref, *, n_blocks):
    qi = pl.program_id(1)
    nbp = _padded_block_rows(n_blocks)

    @pl.when(qi == 0)
    def _():
        kmean_ref[...] = jnp.zeros_like(kmean_ref)
        wl = lax.broadcasted_iota(jnp.int32, (AT, LANES), 1)
        wr = lax.broadcasted_iota(jnp.int32, (AT, LANES), 0)

        def body(n, _):
            start = pl.multiple_of(n * AT, AT)
            kblk = k_ref[0, pl.ds(start, AT), :]
            kmean_ref[pl.ds(n, 1), :] = jnp.mean(kblk.astype(F32), axis=0, keepdims=True)
            window = jnp.where(wl == POS_LANE, wr, (wl == n).astype(jnp.int32)).astype(F32).astype(BF16)
            kaug_ref[0, pl.ds(start, AT), :] = jnp.concatenate([kblk[:, :LANES], window], axis=1)
            kaug_ref[1, pl.ds(start, AT), :] = jnp.concatenate([window, kblk[:, LANES:]], axis=1)
            vt_ref[n] = v_ref[0, pl.ds(start, AT), :].astype(F32).T.astype(BF16)
            return 0
        lax.fori_loop(0, n_blocks, body, 0)
        km = kmean_ref[0:nbp, :]
        masks = _head_masks(nbp)
        for h in range(N_AH):
            kmh = jnp.where(masks[h], km, 0.0)
            hi = kmh.astype(BF16)
            kmh_ref[h] = hi
            kmh_ref[N_AH + h] = (kmh - hi.astype(F32)).astype(BF16)

    qt = (q_ref[0].astype(F32) * QK_SCALE).T
    qt16 = qt.astype(BF16)
    row = lax.broadcasted_iota(jnp.int32, (LANES, AT), 0)
    blk = lax.broadcasted_iota(jnp.int32, (nbp, AT), 0)
    blk_f = blk.astype(F32)
    tail_row = lax.broadcasted_iota(jnp.int32, (LANES - nbp, AT), 0) + nbp
    for h in range(N_AH):
        gate = _dot(kmh_ref[h], qt16) + _dot(kmh_ref[N_AH + h], qt16)
        gate = jnp.where(blk < qi, gate, -jnp.inf)
        chosen = jnp.zeros((nbp, AT), F32)
        for _ in range(MOBA_TOPK):
            m = jnp.max(gate, axis=0, keepdims=True)
            hit = jnp.where(gate == m, blk_f, float(LANES))
            first = jnp.min(hit, axis=0, keepdims=True)
            pick = (blk_f == first) & (m > -jnp.inf)
            chosen = jnp.where(pick, 1.0, chosen)
            gate = jnp.where(pick, -jnp.inf, gate)
        bias = jnp.where((chosen > 0.5) | (blk == qi), 0.0, NEG_BIG)
        tail = jnp.where(tail_row == POS_LANE, MOBA_SLOPES[h], 0.0)
        half, sub = divmod(h, 2)
        q_half = qt[half * LANES:(half + 1) * LANES, :]
        q_half = jnp.where((row >= sub * HEAD_DIM) & (row < (sub + 1) * HEAD_DIM), q_half, 0.0)
        cols = slice(h * AT, (h + 1) * AT)
        q0, b0 = (0, LANES) if half == 0 else (LANES, 0)
        w_ref[q0:q0 + LANES, cols] = q_half.astype(BF16)
        w_ref[b0:b0 + nbp, cols] = bias.astype(BF16)
        w_ref[b0 + nbp:b0 + LANES, cols] = tail.astype(BF16)

    key = lax.broadcasted_iota(jnp.int32, (AT, AT), 0)
    qry = lax.broadcasted_iota(jnp.int32, (AT, AT), 1)
    causal = key <= qry
    heads = [slice(h * AT, (h + 1) * AT) for h in range(N_AH)]
    head_rows = [slice(h * HEAD_DIM, (h + 1) * HEAD_DIM) for h in range(N_AH)]

    def scores(kb, s_ref):
        start = pl.multiple_of(kb * AT, AT)
        for half in range(2):
            cols = slice(2 * half * AT, 2 * (half + 1) * AT)
            s_ref[:, cols] = _dot(kaug_ref[half, pl.ds(start, AT), :], w_ref[:, cols])

    def update(kb, s_ref, stats, own=False):
        ms, ls = stats
        dist = ((kb - qi) * AT).astype(F32)
        nm, nl, scale, pv = [], [], [], []
        for h in range(N_AH):
            s = s_ref[:, heads[h]]
            if own:
                s = jnp.where(causal, s, NEG_BIG)
                m_new = jnp.max(s, axis=0, keepdims=True)
                p = jnp.exp(s - m_new)
                l_new = jnp.sum(p, axis=0, keepdims=True)
            else:
                c = MOBA_SLOPES[h] * dist
                m_new = jnp.maximum(ms[h], jnp.max(s, axis=0, keepdims=True) + c)
                alpha = jnp.exp(ms[h] - m_new)
                p = jnp.exp(s - (m_new - c))
                l_new = alpha * ls[h] + jnp.sum(p, axis=0, keepdims=True)
                scale.append(jnp.broadcast_to(alpha, (HEAD_DIM, AT)))
            nm.append(m_new)
            nl.append(l_new)
            pv.append(_dot(vt_ref[kb, head_rows[h], :], p.astype(BF16)))
        pv = jnp.concatenate(pv, axis=0)
        if own:
            acc_ref[...] = pv
        else:
            acc_ref[...] = acc_ref[...] * jnp.concatenate(scale, axis=0) + pv
        return tuple(nm), tuple(nl)

    scores(qi, sb_ref)
    scores(0, sa_ref)
    stats = update(qi, sb_ref, (None, None), own=True)

    def two_blocks(j, stats):
        kb = 2 * j
        scores(kb + 1, sb_ref)
        stats = update(kb, sa_ref, stats)
        scores(jnp.minimum(kb + 2, qi), sa_ref)
        return update(kb + 1, sb_ref, stats)

    stats = lax.fori_loop(0, qi // 2, two_blocks, stats)
    for h in range(N_AH):
        l_ref[h:h + 1, :] = stats[1][h]

    @pl.when(qi % 2 == 1)
    def _():
        _, ls_odd = update(qi - 1, sa_ref, stats)
        for h in range(N_AH):
            l_ref[h:h + 1, :] = ls_odd[h]

    yt = jnp.concatenate([acc_ref[head_rows[h], :] / l_ref[h:h + 1, :] for h in range(N_AH)], axis=0)
    o_ref[0] = (_rms(yt.T) * nw_ref[...]).astype(BF16)


def _moba_attention(proj, norm_w):
    b, s, _ = proj.shape
    n_blocks = s // AT
    assert s % AT == 0 and n_blocks < POS_LANE
    cb = COL_MOBA // ATT_W
    return pl.pallas_call(
        functools.partial(_moba_kernel, n_blocks=n_blocks),
        grid=(b, n_blocks),
        in_specs=[
            pl.BlockSpec((1, AT, ATT_W), lambda i, j: (i, j, cb)),
            pl.BlockSpec((1, s, ATT_W), lambda i, j: (i, 0, cb + 1)),
            pl.BlockSpec((1, s, ATT_W), lambda i, j: (i, 0, cb + 2)),
            _const_spec((1, ATT_W)),
        ],
        out_specs=pl.BlockSpec((1, AT, ATT_W), lambda i, j: (i, j, 0)),
        out_shape=jax.ShapeDtypeStruct((b, s, ATT_W), BF16),
        scratch_shapes=[
            pltpu.VMEM((ATT_W, N_AH * AT), BF16),
            pltpu.VMEM((2, s, ATT_W), BF16),
            pltpu.VMEM((n_blocks, ATT_W, AT), BF16),
            pltpu.VMEM((LANES, ATT_W), F32),
            pltpu.VMEM((2 * N_AH, _padded_block_rows(n_blocks), ATT_W), BF16),
            pltpu.VMEM((AT, N_AH * AT), F32),
            pltpu.VMEM((AT, N_AH * AT), F32),
            pltpu.VMEM((ATT_W, AT), F32),
            pltpu.VMEM((2 * N_AH, AT), F32),
        ],
        compiler_params=_params(2),
        name="moba_attn",
    )(proj, proj, proj, norm_w.astype(F32).reshape(1, ATT_W))


OUT_TM = 1024
MXU_TILE = 256
_FFN_SPLIT = (FFN_HIDDEN // MXU_TILE + 1) // 2 * MXU_TILE
FFN_CHUNKS = ((0, _FFN_SPLIT), (_FFN_SPLIT, FFN_HIDDEN))


def _out_ffn_kernel(x_ref, ys_ref, yb_ref, ym_ref, wo_ref, n1_ref, n2_ref, wg_ref, wu_ref, wd_ref, n3_ref,
                    o_ref):
    mix = (_dot(ys_ref[0], wo_ref[0:SSD_WIDTH, :])
           + _dot(yb_ref[0], wo_ref[SSD_WIDTH:SSD_WIDTH + SB_WIDTH, :])
           + _dot(ym_ref[0], wo_ref[SSD_WIDTH + SB_WIDTH:, :]))
    x1 = x_ref[0] + _rms(mix) * n1_ref[...]
    h2 = (_rms(x1) * n2_ref[...]).astype(BF16)
    f = None
    for lo, hi in FFN_CHUNKS:
        sl = slice(lo, hi)
        a = (_silu(_dot(h2, wg_ref[:, sl])) * _dot(h2, wu_ref[:, sl])).astype(BF16)
        part = _dot(a, wd_ref[sl, :])
        f = part if f is None else f + part
    o_ref[0] = x1 + _rms(f) * n3_ref[...]


def _out_ffn(x, y_ssd, y_sb, y_moba, w_out, n1, n2, w_gate, w_up, w_down, n3):
    b, s, d = x.shape
    tm = min(OUT_TM, s)
    tile = lambda w: pl.BlockSpec((1, tm, w), lambda i, j: (i, j, 0))
    resident = lambda shape: pl.BlockSpec(shape, lambda i, j: (0, 0), pipeline_mode=pl.Buffered(1))
    vec = lambda v: v.astype(F32).reshape(1, d)
    return pl.pallas_call(
        _out_ffn_kernel,
        grid=(b, s // tm),
        in_specs=[
            tile(d), tile(SSD_WIDTH), tile(SB_WIDTH), tile(MOBA_WIDTH),
            resident((d, d)), _const_spec((1, d)), _const_spec((1, d)),
            resident((d, FFN_HIDDEN)), resident((d, FFN_HIDDEN)), resident((FFN_HIDDEN, d)),
            _const_spec((1, d)),
        ],
        out_specs=tile(d),
        out_shape=jax.ShapeDtypeStruct((b, s, d), F32),
        compiler_params=_params(2),
        name="out_ffn",
    )(x, y_ssd, y_sb, y_moba, w_out, vec(n1), vec(n2), w_gate, w_up, w_down, vec(n3))


def _layer(x, pre_mix_norm, w_in, conv_w, conv_b, dt_bias, a_log, d_skip, ssd_norm, sb_norm, moba_norm,
           w_out, post_mix_norm, pre_ffn_norm, w_gate, w_up, w_down, post_ffn_norm):
    o1 = SSD_WIDTH
    o2 = o1 + CONV_CH
    o3 = o2 + SSD_HEADS
    w_main = jnp.concatenate([w_in[:, o1:o2], w_in[:, :o1], w_in[:, o3:]], axis=1).astype(BF16)
    w_dt = w_in[:, o2:o3].astype(BF16)
    proj, dtT = _in_proj(x, pre_mix_norm.astype(F32).reshape(1, D_MODEL), w_main, w_dt.T)
    y_ssd = _ssd(proj, dtT, conv_w, conv_b, dt_bias, a_log, d_skip, ssd_norm)
    y_sb = _sb_attention(proj, sb_norm)
    y_moba = _moba_attention(proj, moba_norm)
    return _out_ffn(x, y_ssd, y_sb, y_moba, w_out.astype(BF16), post_mix_norm, pre_ffn_norm,
                    w_gate.astype(BF16), w_up.astype(BF16), w_down.astype(BF16), post_ffn_norm)


def kernel(x, pre_mix_norm, w_in, conv_w, conv_b, dt_bias, a_log, d_skip, ssd_norm, sb_norm, moba_norm, w_out,
           post_mix_norm, pre_ffn_norm, w_gate, w_up, w_down, post_ffn_norm):
    depth = w_in.shape[0]
    for l in range(depth):
        x = _layer(x, pre_mix_norm[l], w_in[l], conv_w[l], conv_b[l], dt_bias[l], a_log[l], d_skip[l],
                   ssd_norm[l], sb_norm[l], moba_norm[l], w_out[l], post_mix_norm[l], pre_ffn_norm[l],
                   w_gate[l], w_up[l], w_down[l], post_ffn_norm[l])
    return x
```

```python
import functools
import math

import jax
import jax.numpy as jnp
from jax import lax
from jax.experimental import pallas as pl
from jax.experimental.pallas import tpu as pltpu

F32 = jnp.float32
BF16 = jnp.bfloat16

D_MODEL = 1024
HEAD_DIM = 64
SSD_WIDTH = 512
SSD_HEADS = 8
SSD_GROUPS = 2
SSD_STATE = 128
SSD_CONV = 4
SSD_CHUNK = 128
SB_WIDTH = 256
SB_HEADS = 4
MOBA_WIDTH = 256
MOBA_HEADS = 4
MOBA_BLOCK = 256
MOBA_TOPK = 3
CONV_CH = SSD_WIDTH + 2 * SSD_GROUPS * SSD_STATE
FFN_HIDDEN = 2816
EPS = 1e-6

LANES = 128
BF16_SUBLANES = 16
NEG_BIG = -1e30

COL_XBC = 0
COL_Z = CONV_CH
COL_SB = COL_Z + SSD_WIDTH
COL_MOBA = COL_SB + 3 * SB_WIDTH
PROJ_COLS = COL_MOBA + 3 * MOBA_WIDTH

VMEM_LIMIT = 56 * 1024 * 1024

_NT = (((1,), (1,)), ((), ()))
_TN = (((0,), (0,)), ((), ()))


def _params(n_axes):
    return pltpu.CompilerParams(dimension_semantics=("arbitrary",) * n_axes,
                                vmem_limit_bytes=VMEM_LIMIT)


def _const_spec(shape):
    zeros = (0,) * len(shape)
    return pl.BlockSpec(shape, lambda *_: zeros)


def _dot(a, b):
    return jnp.dot(a, b, preferred_element_type=F32)


def _dot_nt(a, b):
    return lax.dot_general(a, b, _NT, preferred_element_type=F32)


def _dot_tn(a, b):
    return lax.dot_general(a, b, _TN, preferred_element_type=F32)


def _split3(x):
    hi = x.astype(BF16)
    r1 = x - hi.astype(F32)
    mid = r1.astype(BF16)
    lo = (r1 - mid.astype(F32)).astype(BF16)
    return hi, mid, lo


def _dot_exact_rhs(x, m):
    hi, mid, lo = _split3(x)
    return _dot(hi, m) + _dot(mid, m) + _dot(lo, m)


def _softplus(x):
    return jnp.maximum(x, 0.0) + jnp.log(1.0 + jnp.exp(-jnp.abs(x)))


def _silu(x):
    return x / (1.0 + jnp.exp(-x))


def _rms(x):
    return x * lax.rsqrt(jnp.mean(x * x, axis=-1, keepdims=True) + EPS)


IN_TM = 1024
IN_CW = 512


def _in_proj_kernel(x_ref, nw_ref, w_ref, wdtT_ref, proj_ref, dtT_ref):
    x = x_ref[0]
    h = (_rms(x) * nw_ref[...]).astype(BF16)
    for j in range(PROJ_COLS // IN_CW):
        sl = slice(j * IN_CW, (j + 1) * IN_CW)
        proj_ref[0, :, sl] = _dot(h, w_ref[:, sl]).astype(BF16)
    dtT_ref[0] = _dot_nt(wdtT_ref[...], h)


def _in_proj(x, norm_w, w_main, w_dtT):
    b, s, d = x.shape
    tm = min(IN_TM, s)
    return pl.pallas_call(
        _in_proj_kernel,
        grid=(b, s // tm),
        in_specs=[
            pl.BlockSpec((1, tm, d), lambda i, j: (i, j, 0)),
            _const_spec((1, d)),
            _const_spec((d, PROJ_COLS)),
            _const_spec((SSD_HEADS, d)),
        ],
        out_specs=[
            pl.BlockSpec((1, tm, PROJ_COLS), lambda i, j: (i, j, 0)),
            pl.BlockSpec((1, SSD_HEADS, tm), lambda i, j: (i, 0, j)),
        ],
        out_shape=[
            jax.ShapeDtypeStruct((b, s, PROJ_COLS), BF16),
            jax.ShapeDtypeStruct((b, SSD_HEADS, s), F32),
        ],
        compiler_params=_params(2),
        name="in_proj",
    )(x, norm_w, w_main, w_dtT)


Q = SSD_CHUNK
TAIL = 16
SSD_STEP_CHUNKS = 4
GW = SSD_WIDTH // SSD_GROUPS
GN = SSD_GROUPS * SSD_STATE


def _ssd_kernel(xbc_ref, z_ref, dtT_ref, cw_ref, cb_ref, dtbT_ref, alogT_ref,
                dskip_ref, nw_ref, triu_ref, e64_ref, e128_ref, shift_ref,
                y_ref, ext_ref, state_ref):
    c = pl.program_id(1)

    @pl.when(c == 0)
    def _():
        ext_ref[0:TAIL, :] = jnp.zeros((TAIL, CONV_CH), BF16)
        state_ref[...] = jnp.zeros_like(state_ref)

    ext_ref[TAIL:, :] = xbc_ref[0]
    for ci in range(SSD_STEP_CHUNKS):
        _ssd_chunk(ci * Q, z_ref, dtT_ref, cw_ref, cb_ref, dtbT_ref, alogT_ref,
                   dskip_ref, nw_ref, triu_ref, e64_ref, e128_ref, shift_ref, y_ref, ext_ref,
                   state_ref)
    ext_ref[0:TAIL, :] = ext_ref[SSD_STEP_CHUNKS * Q:, :]


def _ssd_chunk(r0, z_ref, dtT_ref, cw_ref, cb_ref, dtbT_ref, alogT_ref,
               dskip_ref, nw_ref, triu_ref, e64_ref, e128_ref, shift_ref, y_ref, ext_ref, state_ref):
    rows = slice(r0, r0 + Q)
    taps = _dot(shift_ref[...], ext_ref[r0:r0 + TAIL + Q, :])
    conv = cb_ref[...] + cw_ref[SSD_CONV - 1:SSD_CONV, :] * ext_ref[r0 + TAIL:r0 + TAIL + Q, :].astype(F32)
    for k in range(SSD_CONV - 1):
        conv = conv + cw_ref[k:k + 1, :] * taps[k * Q:(k + 1) * Q, :]
    xa = _silu(conv)
    xs = xa[:, :SSD_WIDTH]
    bm = xa[:, SSD_WIDTH:SSD_WIDTH + GN]
    cm = xa[:, SSD_WIDTH + GN:]

    dtpT = _softplus(dtT_ref[0, :, rows] + dtbT_ref[...])
    adtT = dtpT * (-jnp.exp(alogT_ref[...]))
    acumT = _dot_exact_rhs(adtT, triu_ref[...])
    pad = jnp.zeros((BF16_SUBLANES - SSD_HEADS, Q), F32)
    dt_exp = _dot_tn(jnp.concatenate([dtpT, pad], axis=0).astype(BF16), e64_ref[...])
    acum_b = sum(_dot_tn(piece, e128_ref[...])
                 for piece in _split3(jnp.concatenate([acumT, pad], axis=0)))
    lo_half = lax.broadcasted_iota(jnp.int32, (Q, LANES), 1) < HEAD_DIM
    acum_exp = jnp.concatenate(
        [jnp.where(lo_half, acum_b[:, 2 * p * LANES:(2 * p + 1) * LANES],
                   acum_b[:, (2 * p + 1) * LANES:(2 * p + 2) * LANES]) for p in range(SSD_HEADS // 2)], axis=1)

    xdt = xs * dt_exp
    acum_last = acum_exp[Q - 1:Q, :]
    xdd = (xdt * jnp.exp(acum_last - acum_exp)).astype(BF16)
    in_decay = jnp.exp(acum_exp)
    chunk_decay = jnp.exp(acum_last)

    row = lax.broadcasted_iota(jnp.int32, (Q, Q), 0)
    col = lax.broadcasted_iota(jnp.int32, (Q, Q), 1)
    causal = row >= col

    bm16 = bm.astype(BF16)
    cm16 = cm.astype(BF16)
    y_parts = []
    for g in range(SSD_GROUPS):
        bg = bm16[:, g * SSD_STATE:(g + 1) * SSD_STATE]
        cg = cm16[:, g * SSD_STATE:(g + 1) * SSD_STATE]
        cb = _dot_nt(cg, bg)
        gsl = slice(g * GW, (g + 1) * GW)
        st = state_ref[g]
        y_off = _dot(cg, st.astype(BF16)) * in_decay[:, gsl]
        s_new = lax.dot_general(bg, xdd[:, gsl], _TN, preferred_element_type=F32)
        state_ref[g] = st * chunk_decay[:, gsl] + s_new
        for pp in range(2):
            p = g * 2 + pp
            ms = []
            for hh in range(2):
                h = p * 2 + hh
                seg = acum_b[:, h * LANES:(h + 1) * LANES] - acumT[h:h + 1, :]
                lmat = jnp.exp(jnp.where(causal, seg, -jnp.inf))
                ms.append((cb * lmat).astype(BF16))
            mcat = jnp.concatenate(ms, axis=1)
            xp = xdt[:, p * LANES:(p + 1) * LANES]
            rhs = jnp.concatenate([jnp.where(lo_half, xp, 0.0), jnp.where(lo_half, 0.0, xp)],
                                  axis=0).astype(BF16)
            y_parts.append(_dot(mcat, rhs) + y_off[:, pp * LANES:(pp + 1) * LANES])
    y = jnp.concatenate(y_parts, axis=1) + dskip_ref[...] * xs
    y = y * _silu(z_ref[0, rows, :].astype(F32))
    y = jnp.concatenate([_rms(y[:, g * GW:(g + 1) * GW]) for g in range(SSD_GROUPS)], axis=1)
    y_ref[0, rows, :] = (y * nw_ref[...]).astype(BF16)


def _ssd(proj, dtT, conv_w, conv_b, dt_bias, a_log, d_skip, norm_w):
    b, s, _ = proj.shape
    r = SSD_STEP_CHUNKS * Q
    col8 = lambda v: v.astype(F32).reshape(SSD_HEADS, 1)
    ii = jnp.arange(Q)
    triu = (ii[:, None] <= ii[None, :]).astype(BF16)
    hh = jnp.arange(BF16_SUBLANES)
    e64 = (hh[:, None] == (jnp.arange(SSD_WIDTH) // HEAD_DIM)[None, :]).astype(BF16)
    e128 = (hh[:, None] == (jnp.arange(SSD_HEADS * LANES) // LANES)[None, :]).astype(BF16)
    dskip = jnp.repeat(d_skip.astype(F32), HEAD_DIM).reshape(1, SSD_WIDTH)
    n_shift = (SSD_CONV - 1) * Q
    tap = jnp.arange(n_shift) // Q
    src = TAIL + (jnp.arange(n_shift) % Q) - (SSD_CONV - 1) + tap
    shift = (src[:, None] == jnp.arange(TAIL + Q)[None, :]).astype(BF16)
    return pl.pallas_call(
        _ssd_kernel,
        grid=(b, s // r),
        in_specs=[
            pl.BlockSpec((1, r, CONV_CH), lambda i, j: (i, j, COL_XBC // CONV_CH)),
            pl.BlockSpec((1, r, SSD_WIDTH), lambda i, j: (i, j, COL_Z // SSD_WIDTH)),
            pl.BlockSpec((1, SSD_HEADS, r), lambda i, j: (i, 0, j)),
            _const_spec((SSD_CONV, CONV_CH)),
            _const_spec((1, CONV_CH)),
            _const_spec((SSD_HEADS, 1)),
            _const_spec((SSD_HEADS, 1)),
            _const_spec((1, SSD_WIDTH)),
            _const_spec((1, SSD_WIDTH)),
            _const_spec((Q, Q)),
            _const_spec((BF16_SUBLANES, SSD_WIDTH)),
            _const_spec((BF16_SUBLANES, SSD_HEADS * LANES)),
            _const_spec(((SSD_CONV - 1) * Q, TAIL + Q)),
        ],
        out_specs=pl.BlockSpec((1, r, SSD_WIDTH), lambda i, j: (i, j, 0)),
        out_shape=jax.ShapeDtypeStruct((b, s, SSD_WIDTH), BF16),
        scratch_shapes=[
            pltpu.VMEM((TAIL + r, CONV_CH), BF16),
            pltpu.VMEM((SSD_GROUPS, SSD_STATE, GW), F32),
        ],
        compiler_params=_params(2),
        name="ssd",
    )(proj, proj, dtT, conv_w.astype(F32), conv_b.astype(F32).reshape(1, CONV_CH),
      col8(dt_bias), col8(a_log), dskip,
      norm_w.astype(F32).reshape(1, SSD_WIDTH), triu, e64, e128, shift)


AT = 256
N_AH = 4
ATT_W = N_AH * HEAD_DIM
QK_SCALE = 1.0 / math.sqrt(HEAD_DIM)
LOG2E = 1.4426950408889634


def _head_masks(rows):
    lane = lax.broadcasted_iota(jnp.int32, (rows, ATT_W), 1)
    return [(lane >= h * HEAD_DIM) & (lane < (h + 1) * HEAD_DIM) for h in range(N_AH)]


def _sb_kernel(q_ref, k_ref, v_ref, nw_ref, ut_ref, o_ref, wq_ref, vt_ref, za_ref, zb_ref, acc_ref,
               carry_ref, *, n_blocks):
    qi = pl.program_id(1)

    @pl.when(qi == 0)
    def _():
        def body(n, _):
            start = pl.multiple_of(n * AT, AT)
            vt_ref[n] = v_ref[0, pl.ds(start, AT), :].astype(F32).T.astype(BF16)
            return 0
        lax.fori_loop(0, n_blocks, body, 0)

    qt = (q_ref[0].astype(F32) * (QK_SCALE * LOG2E)).T
    lane_row = lax.broadcasted_iota(jnp.int32, (ATT_W, AT), 0)
    heads = [slice(h * AT, (h + 1) * AT) for h in range(N_AH)]
    head_rows = [slice(h * HEAD_DIM, (h + 1) * HEAD_DIM) for h in range(N_AH)]
    for h in range(N_AH):
        own = (lane_row >= h * HEAD_DIM) & (lane_row < (h + 1) * HEAD_DIM)
        wq_ref[:, heads[h]] = jnp.where(own, qt, 0.0).astype(BF16)
    key = lax.broadcasted_iota(jnp.int32, (AT, N_AH * AT), 0)
    qry = lax.broadcasted_iota(jnp.int32, (AT, N_AH * AT), 1) & (AT - 1)
    strict = key < qry

    def scores(kb, z_ref):
        start = pl.multiple_of(kb * AT, AT)
        z_ref[...] = _dot(k_ref[0, pl.ds(start, AT), :], wq_ref[...])

    def step(kb, z_ref, carry, diag=False):
        z = z_ref[...]
        pos = jnp.maximum(z, 0.0)
        sp = pos + jnp.log(1.0 + jnp.exp2((z - pos) - pos)) * LOG2E
        if diag:
            sp = jnp.where(strict, sp, 0.0)
        cs = _dot(ut_ref[...], sp.astype(BF16))
        logw = z - sp - cs
        if diag:
            logw = jnp.where(strict, logw, NEG_BIG)
        w = jnp.exp2(logw).astype(BF16)
        later = jnp.exp2(-carry)
        pv = jnp.concatenate(
            [_dot(vt_ref[kb, head_rows[h], :], w[:, heads[h]]) * later[:, heads[h]] for h in range(N_AH)],
            axis=0)
        if diag:
            acc_ref[...] = pv
        else:
            acc_ref[...] += pv
        return carry + cs[0:1, :] + sp[0:1, :]

    scores(qi, zb_ref)
    scores(jnp.maximum(qi - 1, 0), za_ref)
    carry = step(qi, zb_ref, jnp.zeros((1, N_AH * AT), F32), diag=True)

    def two_blocks(kb, carry):
        scores(kb - 1, zb_ref)
        carry = step(kb, za_ref, carry)
        scores(jnp.maximum(kb - 2, 0), za_ref)
        return step(kb - 1, zb_ref, carry)

    carry_ref[0:1, :] = lax.fori_loop(
        0, qi // 4, lambda j, cr: two_blocks(qi - 3 - 4 * j, two_blocks(qi - 1 - 4 * j, cr)), carry)
    left = qi % 4

    @pl.when(left >= 2)
    def _():
        carry_ref[0:1, :] = two_blocks(left - 1, carry_ref[0:1, :])

    @pl.when(left % 2 == 1)
    def _():
        step(0, za_ref, carry_ref[0:1, :])

    o_ref[0] = (_rms(acc_ref[...].T) * nw_ref[...]).astype(BF16)


def _sb_attention(proj, norm_w):
    b, s, _ = proj.shape
    n_blocks = s // AT
    ii = jnp.arange(AT)
    ut = (ii[:, None] < ii[None, :]).astype(BF16)
    cb = COL_SB // ATT_W
    return pl.pallas_call(
        functools.partial(_sb_kernel, n_blocks=n_blocks),
        grid=(b, n_blocks),
        in_specs=[
            pl.BlockSpec((1, AT, ATT_W), lambda i, j: (i, j, cb)),
            pl.BlockSpec((1, s, ATT_W), lambda i, j: (i, 0, cb + 1)),
            pl.BlockSpec((1, s, ATT_W), lambda i, j: (i, 0, cb + 2)),
            _const_spec((1, ATT_W)),
            _const_spec((AT, AT)),
        ],
        out_specs=pl.BlockSpec((1, AT, ATT_W), lambda i, j: (i, j, 0)),
        out_shape=jax.ShapeDtypeStruct((b, s, ATT_W), BF16),
        scratch_shapes=[
            pltpu.VMEM((ATT_W, N_AH * AT), BF16),
            pltpu.VMEM((n_blocks, ATT_W, AT), BF16),
            pltpu.VMEM((AT, N_AH * AT), F32),
            pltpu.VMEM((AT, N_AH * AT), F32),
            pltpu.VMEM((ATT_W, AT), F32),
            pltpu.VMEM((8, N_AH * AT), F32),
        ],
        compiler_params=_params(2),
        name="sb_attn",
    )(proj, proj, proj, norm_w.astype(F32).reshape(1, ATT_W), ut)


MOBA_SLOPES = tuple(2.0 ** (-8.0 * i / MOBA_HEADS) for i in range(1, MOBA_HEADS + 1))
POS_LANE = LANES - 1


def _padded_block_rows(n_blocks):
    return -(-n_blocks // BF16_SUBLANES) * BF16_SUBLANES


def _moba_kernel(q_ref, k_ref, v_ref, nw_ref, o_ref, w_ref, kaug_ref, vt_ref, kmean_ref, kmh_ref, sa_ref,
                 sb_ref, acc_ref, ml_ref, *, n_blocks):
    qi = pl.program_id(1)
    nbp = _padded_block_rows(n_blocks)

    @pl.when(qi == 0)
    def _():
        kmean_ref[...] = jnp.zeros_like(kmean_ref)
        wl = lax.broadcasted_iota(jnp.int32, (AT, LANES), 1)
        wr = lax.broadcasted_iota(jnp.int32, (AT, LANES), 0)

        def body(n, _):
            start = pl.multiple_of(n * AT, AT)
            kblk = k_ref[0, pl.ds(start, AT), :]
            kmean_ref[pl.ds(n, 1), :] = jnp.mean(kblk.astype(F32), axis=0, keepdims=True)
            window = jnp.where(wl == POS_LANE, wr, (wl == n).astype(jnp.int32)).astype(F32).astype(BF16)
            kaug_ref[0, pl.ds(start, AT), :] = jnp.concatenate([kblk[:, :LANES], window], axis=1)
            kaug_ref[1, pl.ds(start, AT), :] = jnp.concatenate([window, kblk[:, LANES:]], axis=1)
            vt_ref[n] = v_ref[0, pl.ds(start, AT), :].astype(F32).T.astype(BF16)
            return 0
        lax.fori_loop(0, n_blocks, body, 0)
        km = kmean_ref[0:nbp, :]
        masks = _head_masks(nbp)
        for h in range(N_AH):
            kmh = jnp.where(masks[h], km, 0.0)
            hi = kmh.astype(BF16)
            kmh_ref[h] = hi
            kmh_ref[N_AH + h] = (kmh - hi.astype(F32)).astype(BF16)

    qt = (q_ref[0].astype(F32) * QK_SCALE).T
    qt16 = qt.astype(BF16)
    row = lax.broadcasted_iota(jnp.int32, (LANES, AT), 0)
    blk = lax.broadcasted_iota(jnp.int32, (nbp, AT), 0)
    blk_f = blk.astype(F32)
    tail_row = lax.broadcasted_iota(jnp.int32, (LANES - nbp, AT), 0) + nbp
    for h in range(N_AH):
        gate = _dot(kmh_ref[h], qt16) + _dot(kmh_ref[N_AH + h], qt16)
        gate = jnp.where(blk < qi, gate, -jnp.inf)
        chosen = jnp.zeros((nbp, AT), F32)
        for _ in range(MOBA_TOPK):
            m = jnp.max(gate, axis=0, keepdims=True)
            hit = jnp.where(gate == m, blk_f, float(LANES))
            first = jnp.min(hit, axis=0, keepdims=True)
            pick = (blk_f == first) & (m > -jnp.inf)
            chosen = jnp.where(pick, 1.0, chosen)
            gate = jnp.where(pick, -jnp.inf, gate)
        bias = jnp.where((chosen > 0.5) | (blk == qi), 0.0, NEG_BIG)
        tail = jnp.where(tail_row == POS_LANE, MOBA_SLOPES[h], 0.0)
        half, sub = divmod(h, 2)
        q_half = qt[half * LANES:(half + 1) * LANES, :]
        q_half = jnp.where((row >= sub * HEAD_DIM) & (row < (sub + 1) * HEAD_DIM), q_half, 0.0)
        cols = slice(h * AT, (h + 1) * AT)
        q0, b0 = (0, LANES) if half == 0 else (LANES, 0)
        w_ref[q0:q0 + LANES, cols] = q_half.astype(BF16)
        w_ref[b0:b0 + nbp, cols] = bias.astype(BF16)
        w_ref[b0 + nbp:b0 + LANES, cols] = tail.astype(BF16)

    key = lax.broadcasted_iota(jnp.int32, (AT, AT), 0)
    qry = lax.broadcasted_iota(jnp.int32, (AT, AT), 1)
    causal = key <= qry
    heads = [slice(h * AT, (h + 1) * AT) for h in range(N_AH)]
    head_rows = [slice(h * HEAD_DIM, (h + 1) * HEAD_DIM) for h in range(N_AH)]

    def scores(kb, s_ref):
        start = pl.multiple_of(kb * AT, AT)
        for half in range(2):
            cols = slice(2 * half * AT, 2 * (half + 1) * AT)
            s_ref[:, cols] = _dot(kaug_ref[half, pl.ds(start, AT), :], w_ref[:, cols])

    def update(kb, s_ref, stats, own=False):
        ms, ls = stats
        dist = ((kb - qi) * AT).astype(F32)
        nm, nl, scale, pv = [], [], [], []
        for h in range(N_AH):
            s = s_ref[:, heads[h]]
            if own:
                s = jnp.where(causal, s, NEG_BIG)
                m_new = jnp.max(s, axis=0, keepdims=True)
                p = jnp.exp(s - m_new)
                l_new = jnp.sum(p, axis=0, keepdims=True)
            else:
                c = MOBA_SLOPES[h] * dist
                m_new = jnp.maximum(ms[h], jnp.max(s, axis=0, keepdims=True) + c)
                alpha = jnp.exp(ms[h] - m_new)
                p = jnp.exp(s - (m_new - c))
                l_new = alpha * ls[h] + jnp.sum(p, axis=0, keepdims=True)
                scale.append(jnp.broadcast_to(alpha, (HEAD_DIM, AT)))
            nm.append(m_new)
            nl.append(l_new)
            pv.append(_dot(vt_ref[kb, head_rows[h], :], p.astype(BF16)))
        pv = jnp.concatenate(pv, axis=0)
        if own:
            acc_ref[...] = pv
        else:
            acc_ref[...] = acc_ref[...] * jnp.concatenate(scale, axis=0) + pv
        return tuple(nm), tuple(nl)

    scores(qi, sb_ref)
    scores(0, sa_ref)
    stats = update(qi, sb_ref, (None, None), own=True)

    def two_blocks(kb, stats):
        scores(kb + 1, sb_ref)
        stats = update(kb, sa_ref, stats)
        scores(jnp.minimum(kb + 2, qi), sa_ref)
        return update(kb + 1, sb_ref, stats)

    def save(stats):
        for h in range(N_AH):
            ml_ref[h:h + 1, :] = stats[0][h]
            ml_ref[N_AH + h:N_AH + h + 1, :] = stats[1][h]

    def load():
        return (tuple(ml_ref[h:h + 1, :] for h in range(N_AH)),
                tuple(ml_ref[N_AH + h:N_AH + h + 1, :] for h in range(N_AH)))

    stats = lax.fori_loop(0, qi // 4, lambda j, st: two_blocks(4 * j + 2, two_blocks(4 * j, st)), stats)
    save(stats)
    done = (qi // 4) * 4

    @pl.when(qi - done >= 2)
    def _():
        save(two_blocks(done, load()))

    @pl.when(qi % 2 == 1)
    def _():
        save(update(qi - 1, sa_ref, load()))

    yt = jnp.concatenate(
        [acc_ref[head_rows[h], :] / ml_ref[N_AH + h:N_AH + h + 1, :] for h in range(N_AH)], axis=0)
    o_ref[0] = (_rms(yt.T) * nw_ref[...]).astype(BF16)


def _moba_attention(proj, norm_w):
    b, s, _ = proj.shape
    n_blocks = s // AT
    assert s % AT == 0 and n_blocks < POS_LANE
    cb = COL_MOBA // ATT_W
    return pl.pallas_call(
        functools.partial(_moba_kernel, n_blocks=n_blocks),
        grid=(b, n_blocks),
        in_specs=[
            pl.BlockSpec((1, AT, ATT_W), lambda i, j: (i, j, cb)),
            pl.BlockSpec((1, s, ATT_W), lambda i, j: (i, 0, cb + 1)),
            pl.BlockSpec((1, s, ATT_W), lambda i, j: (i, 0, cb + 2)),
            _const_spec((1, ATT_W)),
        ],
        out_specs=pl.BlockSpec((1, AT, ATT_W), lambda i, j: (i, j, 0)),
        out_shape=jax.ShapeDtypeStruct((b, s, ATT_W), BF16),
        scratch_shapes=[
            pltpu.VMEM((ATT_W, N_AH * AT), BF16),
            pltpu.VMEM((2, s, ATT_W), BF16),
            pltpu.VMEM((n_blocks, ATT_W, AT), BF16),
            pltpu.VMEM((LANES, ATT_W), F32),
            pltpu.VMEM((2 * N_AH, _padded_block_rows(n_blocks), ATT_W), BF16),
            pltpu.VMEM((AT, N_AH * AT), F32),
            pltpu.VMEM((AT, N_AH * AT), F32),
            pltpu.VMEM((ATT_W, AT), F32),
            pltpu.VMEM((2 * N_AH, AT), F32),
        ],
        compiler_params=_params(2),
        name="moba_attn",
    )(proj, proj, proj, norm_w.astype(F32).reshape(1, ATT_W))


OUT_TM = 1024
MXU_TILE = 256
_FFN_SPLIT = (FFN_HIDDEN // MXU_TILE + 1) // 2 * MXU_TILE
FFN_CHUNKS = ((0, _FFN_SPLIT), (_FFN_SPLIT, FFN_HIDDEN))


def _out_ffn_kernel(x_ref, ys_ref, yb_ref, ym_ref, wo_ref, n1_ref, n2_ref, wg_ref, wu_ref, wd_ref, n3_ref,
                    o_ref):
    mix = (_dot(ys_ref[0], wo_ref[0:SSD_WIDTH, :])
           + _dot(yb_ref[0], wo_ref[SSD_WIDTH:SSD_WIDTH + SB_WIDTH, :])
           + _dot(ym_ref[0], wo_ref[SSD_WIDTH + SB_WIDTH:, :]))
    x1 = x_ref[0] + _rms(mix) * n1_ref[...]
    h2 = (_rms(x1) * n2_ref[...]).astype(BF16)
    f = None
    for lo, hi in FFN_CHUNKS:
        sl = slice(lo, hi)
        a = (_silu(_dot(h2, wg_ref[:, sl])) * _dot(h2, wu_ref[:, sl])).astype(BF16)
        part = _dot(a, wd_ref[sl, :])
        f = part if f is None else f + part
    o_ref[0] = x1 + _rms(f) * n3_ref[...]


def _out_ffn(x, y_ssd, y_sb, y_moba, w_out, n1, n2, w_gate, w_up, w_down, n3):
    b, s, d = x.shape
    tm = min(OUT_TM, s)
    tile = lambda w: pl.BlockSpec((1, tm, w), lambda i, j: (i, j, 0))
    resident = lambda shape: pl.BlockSpec(shape, lambda i, j: (0, 0), pipeline_mode=pl.Buffered(1))
    vec = lambda v: v.astype(F32).reshape(1, d)
    return pl.pallas_call(
        _out_ffn_kernel,
        grid=(b, s // tm),
        in_specs=[
            tile(d), tile(SSD_WIDTH), tile(SB_WIDTH), tile(MOBA_WIDTH),
            resident((d, d)), _const_spec((1, d)), _const_spec((1, d)),
            resident((d, FFN_HIDDEN)), resident((d, FFN_HIDDEN)), resident((FFN_HIDDEN, d)),
            _const_spec((1, d)),
        ],
        out_specs=tile(d),
        out_shape=jax.ShapeDtypeStruct((b, s, d), F32),
        compiler_params=_params(2),
        name="out_ffn",
    )(x, y_ssd, y_sb, y_moba, w_out, vec(n1), vec(n2), w_gate, w_up, w_down, vec(n3))


def _layer(x, pre_mix_norm, w_in, conv_w, conv_b, dt_bias, a_log, d_skip, ssd_norm, sb_norm, moba_norm,
           w_out, post_mix_norm, pre_ffn_norm, w_gate, w_up, w_down, post_ffn_norm):
    o1 = SSD_WIDTH
    o2 = o1 + CONV_CH
    o3 = o2 + SSD_HEADS
    w_main = jnp.concatenate([w_in[:, o1:o2], w_in[:, :o1], w_in[:, o3:]], axis=1).astype(BF16)
    w_dt = w_in[:, o2:o3].astype(BF16)
    proj, dtT = _in_proj(x, pre_mix_norm.astype(F32).reshape(1, D_MODEL), w_main, w_dt.T)
    y_ssd = _ssd(proj, dtT, conv_w, conv_b, dt_bias, a_log, d_skip, ssd_norm)
    y_sb = _sb_attention(proj, sb_norm)
    y_moba = _moba_attention(proj, moba_norm)
    return _out_ffn(x, y_ssd, y_sb, y_moba, w_out.astype(BF16), post_mix_norm, pre_ffn_norm,
                    w_gate.astype(BF16), w_up.astype(BF16), w_down.astype(BF16), post_ffn_norm)


def kernel(x, pre_mix_norm, w_in, conv_w, conv_b, dt_bias, a_log, d_skip, ssd_norm, sb_norm, moba_norm, w_out,
           post_mix_norm, pre_ffn_norm, w_gate, w_up, w_down, post_ffn_norm):
    depth = w_in.shape[0]
    for l in range(depth):
        x = _layer(x, pre_mix_norm[l], w_in[l], conv_w[l], conv_b[l], dt_bias[l], a_log[l], d_skip[l],
                   ssd_norm[l], sb_norm[l], moba_norm[l], w_out[l], post_mix_norm[l], pre_ffn_norm[l],
                   w_gate[l], w_up[l], w_down[l], post_ffn_norm[l])
    return x
```

```python
import functools
import math

import jax
import jax.numpy as jnp
from jax import lax
from jax.experimental import pallas as pl
from jax.experimental.pallas import tpu as pltpu

F32 = jnp.float32
BF16 = jnp.bfloat16

D_MODEL = 1024
HEAD_DIM = 64
SSD_WIDTH = 512
SSD_HEADS = 8
SSD_GROUPS = 2
SSD_STATE = 128
SSD_CONV = 4
SSD_CHUNK = 128
SB_WIDTH = 256
SB_HEADS = 4
MOBA_WIDTH = 256
MOBA_HEADS = 4
MOBA_BLOCK = 256
MOBA_TOPK = 3
CONV_CH = SSD_WIDTH + 2 * SSD_GROUPS * SSD_STATE
FFN_HIDDEN = 2816
EPS = 1e-6

LANES = 128
BF16_SUBLANES = 16
NEG_BIG = -1e30

COL_XBC = 0
COL_Z = CONV_CH
COL_SB = COL_Z + SSD_WIDTH
COL_MOBA = COL_SB + 3 * SB_WIDTH
PROJ_COLS = COL_MOBA + 3 * MOBA_WIDTH

VMEM_LIMIT = 56 * 1024 * 1024

_NT = (((1,), (1,)), ((), ()))
_TN = (((0,), (0,)), ((), ()))


def _params(n_axes):
    return pltpu.CompilerParams(dimension_semantics=("arbitrary",) * n_axes,
                                vmem_limit_bytes=VMEM_LIMIT)


def _const_spec(shape):
    zeros = (0,) * len(shape)
    return pl.BlockSpec(shape, lambda *_: zeros)


def _dot(a, b):
    return jnp.dot(a, b, preferred_element_type=F32)


def _dot_nt(a, b):
    return lax.dot_general(a, b, _NT, preferred_element_type=F32)


def _dot_tn(a, b):
    return lax.dot_general(a, b, _TN, preferred_element_type=F32)


def _split3(x):
    hi = x.astype(BF16)
    r1 = x - hi.astype(F32)
    mid = r1.astype(BF16)
    lo = (r1 - mid.astype(F32)).astype(BF16)
    return hi, mid, lo


def _dot_exact_rhs(x, m):
    hi, mid, lo = _split3(x)
    return _dot(hi, m) + _dot(mid, m) + _dot(lo, m)


def _softplus(x):
    return jnp.maximum(x, 0.0) + jnp.log(1.0 + jnp.exp(-jnp.abs(x)))


def _silu(x):
    return x / (1.0 + jnp.exp(-x))


def _rms(x):
    return x * lax.rsqrt(jnp.mean(x * x, axis=-1, keepdims=True) + EPS)


IN_TM = 1024
IN_CW = 512


def _in_proj_kernel(x_ref, nw_ref, w_ref, wdtT_ref, proj_ref, dtT_ref):
    x = x_ref[0]
    h = (_rms(x) * nw_ref[...]).astype(BF16)
    for j in range(PROJ_COLS // IN_CW):
        sl = slice(j * IN_CW, (j + 1) * IN_CW)
        proj_ref[0, :, sl] = _dot(h, w_ref[:, sl]).astype(BF16)
    dtT_ref[0] = _dot_nt(wdtT_ref[...], h)


def _in_proj(x, norm_w, w_main, w_dtT):
    b, s, d = x.shape
    tm = min(IN_TM, s)
    return pl.pallas_call(
        _in_proj_kernel,
        grid=(b, s // tm),
        in_specs=[
            pl.BlockSpec((1, tm, d), lambda i, j: (i, j, 0)),
            _const_spec((1, d)),
            _const_spec((d, PROJ_COLS)),
            _const_spec((SSD_HEADS, d)),
        ],
        out_specs=[
            pl.BlockSpec((1, tm, PROJ_COLS), lambda i, j: (i, j, 0)),
            pl.BlockSpec((1, SSD_HEADS, tm), lambda i, j: (i, 0, j)),
        ],
        out_shape=[
            jax.ShapeDtypeStruct((b, s, PROJ_COLS), BF16),
            jax.ShapeDtypeStruct((b, SSD_HEADS, s), F32),
        ],
        compiler_params=_params(2),
        name="in_proj",
    )(x, norm_w, w_main, w_dtT)


Q = SSD_CHUNK
TAIL = 16
SSD_STEP_CHUNKS = 4
GW = SSD_WIDTH // SSD_GROUPS
GN = SSD_GROUPS * SSD_STATE


def _ssd_kernel(xbc_ref, z_ref, dtT_ref, cw_ref, cb_ref, dtbT_ref, alogT_ref,
                dskip_ref, nw_ref, triu_ref, e64_ref, e128_ref, shift_ref,
                y_ref, ext_ref, state_ref):
    c = pl.program_id(1)

    @pl.when(c == 0)
    def _():
        ext_ref[0:TAIL, :] = jnp.zeros((TAIL, CONV_CH), BF16)
        state_ref[...] = jnp.zeros_like(state_ref)

    ext_ref[TAIL:, :] = xbc_ref[0]
    for ci in range(SSD_STEP_CHUNKS):
        _ssd_chunk(ci * Q, z_ref, dtT_ref, cw_ref, cb_ref, dtbT_ref, alogT_ref,
                   dskip_ref, nw_ref, triu_ref, e64_ref, e128_ref, shift_ref, y_ref, ext_ref,
                   state_ref)
    ext_ref[0:TAIL, :] = ext_ref[SSD_STEP_CHUNKS * Q:, :]


def _ssd_chunk(r0, z_ref, dtT_ref, cw_ref, cb_ref, dtbT_ref, alogT_ref,
               dskip_ref, nw_ref, triu_ref, e64_ref, e128_ref, shift_ref, y_ref, ext_ref, state_ref):
    rows = slice(r0, r0 + Q)
    taps = _dot(shift_ref[...], ext_ref[r0:r0 + TAIL + Q, :])
    conv = cb_ref[...] + cw_ref[SSD_CONV - 1:SSD_CONV, :] * ext_ref[r0 + TAIL:r0 + TAIL + Q, :].astype(F32)
    for k in range(SSD_CONV - 1):
        conv = conv + cw_ref[k:k + 1, :] * taps[k * Q:(k + 1) * Q, :]
    xa = _silu(conv)
    xs = xa[:, :SSD_WIDTH]
    bm = xa[:, SSD_WIDTH:SSD_WIDTH + GN]
    cm = xa[:, SSD_WIDTH + GN:]

    dtpT = _softplus(dtT_ref[0, :, rows] + dtbT_ref[...])
    adtT = dtpT * (-jnp.exp(alogT_ref[...]))
    acumT = _dot_exact_rhs(adtT, triu_ref[...])
    pad = jnp.zeros((BF16_SUBLANES - SSD_HEADS, Q), F32)
    dt_exp = _dot_tn(jnp.concatenate([dtpT, pad], axis=0).astype(BF16), e64_ref[...])
    acum_b = sum(_dot_tn(piece, e128_ref[...])
                 for piece in _split3(jnp.concatenate([acumT, pad], axis=0)))
    lo_half = lax.broadcasted_iota(jnp.int32, (Q, LANES), 1) < HEAD_DIM
    acum_exp = jnp.concatenate(
        [jnp.where(lo_half, acum_b[:, 2 * p * LANES:(2 * p + 1) * LANES],
                   acum_b[:, (2 * p + 1) * LANES:(2 * p + 2) * LANES]) for p in range(SSD_HEADS // 2)], axis=1)

    xdt = xs * dt_exp
    acum_last = acum_exp[Q - 1:Q, :]
    xdd = (xdt * jnp.exp(acum_last - acum_exp)).astype(BF16)
    in_decay = jnp.exp(acum_exp)
    chunk_decay = jnp.exp(acum_last)

    row = lax.broadcasted_iota(jnp.int32, (Q, Q), 0)
    col = lax.broadcasted_iota(jnp.int32, (Q, Q), 1)
    causal = row >= col

    bm16 = bm.astype(BF16)
    cm16 = cm.astype(BF16)
    y_parts = []
    for g in range(SSD_GROUPS):
        bg = bm16[:, g * SSD_STATE:(g + 1) * SSD_STATE]
        cg = cm16[:, g * SSD_STATE:(g + 1) * SSD_STATE]
        cb = _dot_nt(cg, bg)
        gsl = slice(g * GW, (g + 1) * GW)
        st = state_ref[g]
        y_off = _dot(cg, st.astype(BF16)) * in_decay[:, gsl]
        s_new = lax.dot_general(bg, xdd[:, gsl], _TN, preferred_element_type=F32)
        state_ref[g] = st * chunk_decay[:, gsl] + s_new
        for pp in range(2):
            p = g * 2 + pp
            ms = []
            for hh in range(2):
                h = p * 2 + hh
                seg = acum_b[:, h * LANES:(h + 1) * LANES] - acumT[h:h + 1, :]
                lmat = jnp.exp(jnp.where(causal, seg, -jnp.inf))
                ms.append((cb * lmat).astype(BF16))
            mcat = jnp.concatenate(ms, axis=1)
            xp = xdt[:, p * LANES:(p + 1) * LANES]
            rhs = jnp.concatenate([jnp.where(lo_half, xp, 0.0), jnp.where(lo_half, 0.0, xp)],
                                  axis=0).astype(BF16)
            y_parts.append(_dot(mcat, rhs) + y_off[:, pp * LANES:(pp + 1) * LANES])
    y = jnp.concatenate(y_parts, axis=1) + dskip_ref[...] * xs
    y = y * _silu(z_ref[0, rows, :].astype(F32))
    y = jnp.concatenate([_rms(y[:, g * GW:(g + 1) * GW]) for g in range(SSD_GROUPS)], axis=1)
    y_ref[0, rows, :] = (y * nw_ref[...]).astype(BF16)


def _ssd(proj, dtT, conv_w, conv_b, dt_bias, a_log, d_skip, norm_w):
    b, s, _ = proj.shape
    r = SSD_STEP_CHUNKS * Q
    col8 = lambda v: v.astype(F32).reshape(SSD_HEADS, 1)
    ii = jnp.arange(Q)
    triu = (ii[:, None] <= ii[None, :]).astype(BF16)
    hh = jnp.arange(BF16_SUBLANES)
    e64 = (hh[:, None] == (jnp.arange(SSD_WIDTH) // HEAD_DIM)[None, :]).astype(BF16)
    e128 = (hh[:, None] == (jnp.arange(SSD_HEADS * LANES) // LANES)[None, :]).astype(BF16)
    dskip = jnp.repeat(d_skip.astype(F32), HEAD_DIM).reshape(1, SSD_WIDTH)
    n_shift = (SSD_CONV - 1) * Q
    tap = jnp.arange(n_shift) // Q
    src = TAIL + (jnp.arange(n_shift) % Q) - (SSD_CONV - 1) + tap
    shift = (src[:, None] == jnp.arange(TAIL + Q)[None, :]).astype(BF16)
    return pl.pallas_call(
        _ssd_kernel,
        grid=(b, s // r),
        in_specs=[
            pl.BlockSpec((1, r, CONV_CH), lambda i, j: (i, j, COL_XBC // CONV_CH)),
            pl.BlockSpec((1, r, SSD_WIDTH), lambda i, j: (i, j, COL_Z // SSD_WIDTH)),
            pl.BlockSpec((1, SSD_HEADS, r), lambda i, j: (i, 0, j)),
            _const_spec((SSD_CONV, CONV_CH)),
            _const_spec((1, CONV_CH)),
            _const_spec((SSD_HEADS, 1)),
            _const_spec((SSD_HEADS, 1)),
            _const_spec((1, SSD_WIDTH)),
            _const_spec((1, SSD_WIDTH)),
            _const_spec((Q, Q)),
            _const_spec((BF16_SUBLANES, SSD_WIDTH)),
            _const_spec((BF16_SUBLANES, SSD_HEADS * LANES)),
            _const_spec(((SSD_CONV - 1) * Q, TAIL + Q)),
        ],
        out_specs=pl.BlockSpec((1, r, SSD_WIDTH), lambda i, j: (i, j, 0)),
        out_shape=jax.ShapeDtypeStruct((b, s, SSD_WIDTH), BF16),
        scratch_shapes=[
            pltpu.VMEM((TAIL + r, CONV_CH), BF16),
            pltpu.VMEM((SSD_GROUPS, SSD_STATE, GW), F32),
        ],
        compiler_params=_params(2),
        name="ssd",
    )(proj, proj, dtT, conv_w.astype(F32), conv_b.astype(F32).reshape(1, CONV_CH),
      col8(dt_bias), col8(a_log), dskip,
      norm_w.astype(F32).reshape(1, SSD_WIDTH), triu, e64, e128, shift)


AT = 256
N_AH = 4
ATT_W = N_AH * HEAD_DIM
QK_SCALE = 1.0 / math.sqrt(HEAD_DIM)
LOG2E = 1.4426950408889634


def _head_masks(rows):
    lane = lax.broadcasted_iota(jnp.int32, (rows, ATT_W), 1)
    return [(lane >= h * HEAD_DIM) & (lane < (h + 1) * HEAD_DIM) for h in range(N_AH)]


def _sb_kernel(q_ref, k_ref, v_ref, nw_ref, ut_ref, o_ref, wq_ref, vt_ref, za_ref, zb_ref, acc_ref,
               carry_ref, *, n_blocks):
    qi = pl.program_id(1)

    @pl.when(qi == 0)
    def _():
        def body(n, _):
            start = pl.multiple_of(n * AT, AT)
            vt_ref[n] = v_ref[0, pl.ds(start, AT), :].astype(F32).T.astype(BF16)
            return 0
        lax.fori_loop(0, n_blocks, body, 0)

    qt = (q_ref[0].astype(F32) * (QK_SCALE * LOG2E)).T
    lane_row = lax.broadcasted_iota(jnp.int32, (ATT_W, AT), 0)
    heads = [slice(h * AT, (h + 1) * AT) for h in range(N_AH)]
    head_rows = [slice(h * HEAD_DIM, (h + 1) * HEAD_DIM) for h in range(N_AH)]
    for h in range(N_AH):
        own = (lane_row >= h * HEAD_DIM) & (lane_row < (h + 1) * HEAD_DIM)
        wq_ref[:, heads[h]] = jnp.where(own, qt, 0.0).astype(BF16)
    key = lax.broadcasted_iota(jnp.int32, (AT, N_AH * AT), 0)
    qry = lax.broadcasted_iota(jnp.int32, (AT, N_AH * AT), 1) & (AT - 1)
    strict = key < qry

    def scores(kb, z_ref):
        start = pl.multiple_of(kb * AT, AT)
        z_ref[...] = _dot(k_ref[0, pl.ds(start, AT), :], wq_ref[...])

    def step(kb, z_ref, carry, diag=False):
        z = z_ref[...]
        pos = jnp.maximum(z, 0.0)
        sp = pos + jnp.log(1.0 + jnp.exp2((z - pos) - pos)) * LOG2E
        if diag:
            sp = jnp.where(strict, sp, 0.0)
        cs = _dot(ut_ref[...], sp.astype(BF16))
        logw = z - sp - cs
        if diag:
            logw = jnp.where(strict, logw, NEG_BIG)
        w = jnp.exp2(logw).astype(BF16)
        later = jnp.exp2(-carry)
        pv = jnp.concatenate(
            [_dot(vt_ref[kb, head_rows[h], :], w[:, heads[h]]) * later[:, heads[h]] for h in range(N_AH)],
            axis=0)
        if diag:
            acc_ref[...] = pv
        else:
            acc_ref[...] += pv
        return carry + cs[0:1, :] + sp[0:1, :]

    scores(qi, zb_ref)
    scores(jnp.maximum(qi - 1, 0), za_ref)
    carry = step(qi, zb_ref, jnp.zeros((1, N_AH * AT), F32), diag=True)

    def two_blocks(kb, carry):
        scores(kb - 1, zb_ref)
        carry = step(kb, za_ref, carry)
        scores(jnp.maximum(kb - 2, 0), za_ref)
        return step(kb - 1, zb_ref, carry)

    carry_ref[0:1, :] = lax.fori_loop(
        0, qi // 4, lambda j, cr: two_blocks(qi - 3 - 4 * j, two_blocks(qi - 1 - 4 * j, cr)), carry)
    left = qi % 4

    @pl.when(left >= 2)
    def _():
        carry_ref[0:1, :] = two_blocks(left - 1, carry_ref[0:1, :])

    @pl.when(left % 2 == 1)
    def _():
        step(0, za_ref, carry_ref[0:1, :])

    o_ref[0] = (_rms(acc_ref[...].T) * nw_ref[...]).astype(BF16)


def _sb_attention(proj, norm_w):
    b, s, _ = proj.shape
    n_blocks = s // AT
    ii = jnp.arange(AT)
    ut = (ii[:, None] < ii[None, :]).astype(BF16)
    cb = COL_SB // ATT_W
    return pl.pallas_call(
        functools.partial(_sb_kernel, n_blocks=n_blocks),
        grid=(b, n_blocks),
        in_specs=[
            pl.BlockSpec((1, AT, ATT_W), lambda i, j: (i, j, cb)),
            pl.BlockSpec((1, s, ATT_W), lambda i, j: (i, 0, cb + 1)),
            pl.BlockSpec((1, s, ATT_W), lambda i, j: (i, 0, cb + 2)),
            _const_spec((1, ATT_W)),
            _const_spec((AT, AT)),
        ],
        out_specs=pl.BlockSpec((1, AT, ATT_W), lambda i, j: (i, j, 0)),
        out_shape=jax.ShapeDtypeStruct((b, s, ATT_W), BF16),
        scratch_shapes=[
            pltpu.VMEM((ATT_W, N_AH * AT), BF16),
            pltpu.VMEM((n_blocks, ATT_W, AT), BF16),
            pltpu.VMEM((AT, N_AH * AT), F32),
            pltpu.VMEM((AT, N_AH * AT), F32),
            pltpu.VMEM((ATT_W, AT), F32),
            pltpu.VMEM((8, N_AH * AT), F32),
        ],
        compiler_params=_params(2),
        name="sb_attn",
    )(proj, proj, proj, norm_w.astype(F32).reshape(1, ATT_W), ut)


MOBA_SLOPES = tuple(2.0 ** (-8.0 * i / MOBA_HEADS) for i in range(1, MOBA_HEADS + 1))
POS_LANE = LANES - 1
MOBA_TRIP = 8


def _padded_block_rows(n_blocks):
    return -(-n_blocks // BF16_SUBLANES) * BF16_SUBLANES


def _moba_kernel(q_ref, k_ref, v_ref, nw_ref, o_ref, w_ref, kaug_ref, vt_ref, kmean_ref, kmh_ref, sa_ref,
                 sb_ref, acc_ref, ml_ref, *, n_blocks):
    qi = pl.program_id(1)
    nbp = _padded_block_rows(n_blocks)

    @pl.when(qi == 0)
    def _():
        kmean_ref[...] = jnp.zeros_like(kmean_ref)
        wl = lax.broadcasted_iota(jnp.int32, (AT, LANES), 1)
        wr = lax.broadcasted_iota(jnp.int32, (AT, LANES), 0)

        def body(n, _):
            start = pl.multiple_of(n * AT, AT)
            kblk = k_ref[0, pl.ds(start, AT), :]
            kmean_ref[pl.ds(n, 1), :] = jnp.mean(kblk.astype(F32), axis=0, keepdims=True)
            window = jnp.where(wl == POS_LANE, wr, (wl == n).astype(jnp.int32)).astype(F32).astype(BF16)
            kaug_ref[0, pl.ds(start, AT), :] = jnp.concatenate([kblk[:, :LANES], window], axis=1)
            kaug_ref[1, pl.ds(start, AT), :] = jnp.concatenate([window, kblk[:, LANES:]], axis=1)
            vt_ref[n] = v_ref[0, pl.ds(start, AT), :].astype(F32).T.astype(BF16)
            return 0
        lax.fori_loop(0, n_blocks, body, 0)
        km = kmean_ref[0:nbp, :]
        masks = _head_masks(nbp)
        for h in range(N_AH):
            kmh = jnp.where(masks[h], km, 0.0)
            hi = kmh.astype(BF16)
            kmh_ref[h] = hi
            kmh_ref[N_AH + h] = (kmh - hi.astype(F32)).astype(BF16)

    qt = (q_ref[0].astype(F32) * QK_SCALE).T
    qt16 = qt.astype(BF16)
    row = lax.broadcasted_iota(jnp.int32, (LANES, AT), 0)
    blk = lax.broadcasted_iota(jnp.int32, (nbp, AT), 0)
    blk_f = blk.astype(F32)
    tail_row = lax.broadcasted_iota(jnp.int32, (LANES - nbp, AT), 0) + nbp
    for h in range(N_AH):
        gate = _dot(kmh_ref[h], qt16) + _dot(kmh_ref[N_AH + h], qt16)
        gate = jnp.where(blk < qi, gate, -jnp.inf)
        chosen = jnp.zeros((nbp, AT), F32)
        for _ in range(MOBA_TOPK):
            m = jnp.max(gate, axis=0, keepdims=True)
            hit = jnp.where(gate == m, blk_f, float(LANES))
            first = jnp.min(hit, axis=0, keepdims=True)
            pick = (blk_f == first) & (m > -jnp.inf)
            chosen = jnp.where(pick, 1.0, chosen)
            gate = jnp.where(pick, -jnp.inf, gate)
        bias = jnp.where((chosen > 0.5) | (blk == qi), 0.0, NEG_BIG)
        tail = jnp.where(tail_row == POS_LANE, MOBA_SLOPES[h], 0.0)
        half, sub = divmod(h, 2)
        q_half = qt[half * LANES:(half + 1) * LANES, :]
        q_half = jnp.where((row >= sub * HEAD_DIM) & (row < (sub + 1) * HEAD_DIM), q_half, 0.0)
        cols = slice(h * AT, (h + 1) * AT)
        q0, b0 = (0, LANES) if half == 0 else (LANES, 0)
        w_ref[q0:q0 + LANES, cols] = q_half.astype(BF16)
        w_ref[b0:b0 + nbp, cols] = bias.astype(BF16)
        w_ref[b0 + nbp:b0 + LANES, cols] = tail.astype(BF16)

    key = lax.broadcasted_iota(jnp.int32, (AT, AT), 0)
    qry = lax.broadcasted_iota(jnp.int32, (AT, AT), 1)
    causal = key <= qry
    heads = [slice(h * AT, (h + 1) * AT) for h in range(N_AH)]
    head_rows = [slice(h * HEAD_DIM, (h + 1) * HEAD_DIM) for h in range(N_AH)]

    def scores(kb, s_ref):
        start = pl.multiple_of(kb * AT, AT)
        for half in range(2):
            cols = slice(2 * half * AT, 2 * (half + 1) * AT)
            s_ref[:, cols] = _dot(kaug_ref[half, pl.ds(start, AT), :], w_ref[:, cols])

    def update(kb, s_ref, stats, own=False):
        ms, ls = stats
        dist = ((kb - qi) * AT).astype(F32)
        nm, nl, scale, pv = [], [], [], []
        for h in range(N_AH):
            s = s_ref[:, heads[h]]
            if own:
                s = jnp.where(causal, s, NEG_BIG)
                m_new = jnp.max(s, axis=0, keepdims=True)
                p = jnp.exp(s - m_new)
                l_new = jnp.sum(p, axis=0, keepdims=True)
            else:
                c = MOBA_SLOPES[h] * dist
                m_new = jnp.maximum(ms[h], jnp.max(s, axis=0, keepdims=True) + c)
                alpha = jnp.exp(ms[h] - m_new)
                p = jnp.exp(s - (m_new - c))
                l_new = alpha * ls[h] + jnp.sum(p, axis=0, keepdims=True)
                scale.append(jnp.broadcast_to(alpha, (HEAD_DIM, AT)))
            nm.append(m_new)
            nl.append(l_new)
            pv.append(_dot(vt_ref[kb, head_rows[h], :], p.astype(BF16)))
        pv = jnp.concatenate(pv, axis=0)
        if own:
            acc_ref[...] = pv
        else:
            acc_ref[...] = acc_ref[...] * jnp.concatenate(scale, axis=0) + pv
        return tuple(nm), tuple(nl)

    scores(qi, sb_ref)
    scores(0, sa_ref)
    stats = update(qi, sb_ref, (None, None), own=True)

    def two_blocks(kb, stats):
        scores(kb + 1, sb_ref)
        stats = update(kb, sa_ref, stats)
        scores(jnp.minimum(kb + 2, qi), sa_ref)
        return update(kb + 1, sb_ref, stats)

    def save(stats):
        for h in range(N_AH):
            ml_ref[h:h + 1, :] = stats[0][h]
            ml_ref[N_AH + h:N_AH + h + 1, :] = stats[1][h]

    def load():
        return (tuple(ml_ref[h:h + 1, :] for h in range(N_AH)),
                tuple(ml_ref[N_AH + h:N_AH + h + 1, :] for h in range(N_AH)))

    def run(kb, stats, count):
        for i in range(0, count, 2):
            stats = two_blocks(kb + i, stats)
        return stats

    stats = lax.fori_loop(0, qi // MOBA_TRIP, lambda j, st: run(MOBA_TRIP * j, st, MOBA_TRIP), stats)
    save(stats)
    span = MOBA_TRIP // 2
    while span >= 2:
        start = (qi // (2 * span)) * (2 * span)

        @pl.when(qi % (2 * span) >= span)
        def _(start=start, span=span):
            save(run(start, load(), span))
        span //= 2

    @pl.when(qi % 2 == 1)
    def _():
        save(update(qi - 1, sa_ref, load()))

    yt = jnp.concatenate(
        [acc_ref[head_rows[h], :] / ml_ref[N_AH + h:N_AH + h + 1, :] for h in range(N_AH)], axis=0)
    o_ref[0] = (_rms(yt.T) * nw_ref[...]).astype(BF16)


def _moba_attention(proj, norm_w):
    b, s, _ = proj.shape
    n_blocks = s // AT
    assert s % AT == 0 and n_blocks < POS_LANE
    cb = COL_MOBA // ATT_W
    return pl.pallas_call(
        functools.partial(_moba_kernel, n_blocks=n_blocks),
        grid=(b, n_blocks),
        in_specs=[
            pl.BlockSpec((1, AT, ATT_W), lambda i, j: (i, j, cb)),
            pl.BlockSpec((1, s, ATT_W), lambda i, j: (i, 0, cb + 1)),
            pl.BlockSpec((1, s, ATT_W), lambda i, j: (i, 0, cb + 2)),
            _const_spec((1, ATT_W)),
        ],
        out_specs=pl.BlockSpec((1, AT, ATT_W), lambda i, j: (i, j, 0)),
        out_shape=jax.ShapeDtypeStruct((b, s, ATT_W), BF16),
        scratch_shapes=[
            pltpu.VMEM((ATT_W, N_AH * AT), BF16),
            pltpu.VMEM((2, s, ATT_W), BF16),
            pltpu.VMEM((n_blocks, ATT_W, AT), BF16),
            pltpu.VMEM((LANES, ATT_W), F32),
            pltpu.VMEM((2 * N_AH, _padded_block_rows(n_blocks), ATT_W), BF16),
            pltpu.VMEM((AT, N_AH * AT), F32),
            pltpu.VMEM((AT, N_AH * AT), F32),
            pltpu.VMEM((ATT_W, AT), F32),
            pltpu.VMEM((2 * N_AH, AT), F32),
        ],
        compiler_params=_params(2),
        name="moba_attn",
    )(proj, proj, proj, norm_w.astype(F32).reshape(1, ATT_W))


OUT_TM = 1024
MXU_TILE = 256
_FFN_SPLIT = (FFN_HIDDEN // MXU_TILE + 1) // 2 * MXU_TILE
FFN_CHUNKS = ((0, _FFN_SPLIT), (_FFN_SPLIT, FFN_HIDDEN))


def _out_ffn_kernel(x_ref, ys_ref, yb_ref, ym_ref, wo_ref, n1_ref, n2_ref, wg_ref, wu_ref, wd_ref, n3_ref,
                    o_ref):
    mix = (_dot(ys_ref[0], wo_ref[0:SSD_WIDTH, :])
           + _dot(yb_ref[0], wo_ref[SSD_WIDTH:SSD_WIDTH + SB_WIDTH, :])
           + _dot(ym_ref[0], wo_ref[SSD_WIDTH + SB_WIDTH:, :]))
    x1 = x_ref[0] + _rms(mix) * n1_ref[...]
    h2 = (_rms(x1) * n2_ref[...]).astype(BF16)
    f = None
    for lo, hi in FFN_CHUNKS:
        sl = slice(lo, hi)
        a = (_silu(_dot(h2, wg_ref[:, sl])) * _dot(h2, wu_ref[:, sl])).astype(BF16)
        part = _dot(a, wd_ref[sl, :])
        f = part if f is None else f + part
    o_ref[0] = x1 + _rms(f) * n3_ref[...]


def _out_ffn(x, y_ssd, y_sb, y_moba, w_out, n1, n2, w_gate, w_up, w_down, n3):
    b, s, d = x.shape
    tm = min(OUT_TM, s)
    tile = lambda w: pl.BlockSpec((1, tm, w), lambda i, j: (i, j, 0))
    resident = lambda shape: pl.BlockSpec(shape, lambda i, j: (0, 0), pipeline_mode=pl.Buffered(1))
    vec = lambda v: v.astype(F32).reshape(1, d)
    return pl.pallas_call(
        _out_ffn_kernel,
        grid=(b, s // tm),
        in_specs=[
            tile(d), tile(SSD_WIDTH), tile(SB_WIDTH), tile(MOBA_WIDTH),
            resident((d, d)), _const_spec((1, d)), _const_spec((1, d)),
            resident((d, FFN_HIDDEN)), resident((d, FFN_HIDDEN)), resident((FFN_HIDDEN, d)),
            _const_spec((1, d)),
        ],
        out_specs=tile(d),
        out_shape=jax.ShapeDtypeStruct((b, s, d), F32),
        compiler_params=_params(2),
        name="out_ffn",
    )(x, y_ssd, y_sb, y_moba, w_out, vec(n1), vec(n2), w_gate, w_up, w_down, vec(n3))


def _layer(x, pre_mix_norm, w_in, conv_w, conv_b, dt_bias, a_log, d_skip, ssd_norm, sb_norm, moba_norm,
           w_out, post_mix_norm, pre_ffn_norm, w_gate, w_up, w_down, post_ffn_norm):
    o1 = SSD_WIDTH
    o2 = o1 + CONV_CH
    o3 = o2 + SSD_HEADS
    w_main = jnp.concatenate([w_in[:, o1:o2], w_in[:, :o1], w_in[:, o3:]], axis=1).astype(BF16)
    w_dt = w_in[:, o2:o3].astype(BF16)
    proj, dtT = _in_proj(x, pre_mix_norm.astype(F32).reshape(1, D_MODEL), w_main, w_dt.T)
    y_ssd = _ssd(proj, dtT, conv_w, conv_b, dt_bias, a_log, d_skip, ssd_norm)
    y_sb = _sb_attention(proj, sb_norm)
    y_moba = _moba_attention(proj, moba_norm)
    return _out_ffn(x, y_ssd, y_sb, y_moba, w_out.astype(BF16), post_mix_norm, pre_ffn_norm,
                    w_gate.astype(BF16), w_up.astype(BF16), w_down.astype(BF16), post_ffn_norm)


def kernel(x, pre_mix_norm, w_in, conv_w, conv_b, dt_bias, a_log, d_skip, ssd_norm, sb_norm, moba_norm, w_out,
           post_mix_norm, pre_ffn_norm, w_gate, w_up, w_down, post_ffn_norm):
    depth = w_in.shape[0]
    for l in range(depth):
        x = _layer(x, pre_mix_norm[l], w_in[l], conv_w[l], conv_b[l], dt_bias[l], a_log[l], d_skip[l],
                   ssd_norm[l], sb_norm[l], moba_norm[l], w_out[l], post_mix_norm[l], pre_ffn_norm[l],
                   w_gate[l], w_up[l], w_down[l], post_ffn_norm[l])
    return x
```
